```python
import math
import jax, jax.numpy as jnp
from jax import lax
import numpy as np

D_MODEL = 1024
BATCH = 4
SEQ = 4096
DEPTH = 2

HEAD_DIM = 64
GROUP_W = D_MODEL // 4
MIX_W = 4 * GROUP_W
DIFF_HEADS = GROUP_W // HEAD_DIM
DIFF_QK_DIM = HEAD_DIM // 2
CONV_W = GROUP_W
CONV_KERNEL = 31
GMLP_HEADS = GROUP_W // HEAD_DIM
GMLP_W = GROUP_W
CHUNK = 128
FOX_HEADS = GROUP_W // HEAD_DIM
FOX_W = GROUP_W
D_FF = ((8 * D_MODEL // 3 + 127) // 128) * 128
Q_BLOCK = 128
N_MOD = 9
EPS = 1e-6
NEG_INF = -1e30

DIFF_Q0 = 0
DIFF_K0 = DIFF_Q0 + DIFF_HEADS * 2 * DIFF_QK_DIM
DIFF_V0 = DIFF_K0 + DIFF_HEADS * 2 * DIFF_QK_DIM
CONV0 = DIFF_V0 + DIFF_HEADS * HEAD_DIM
GMLP0 = CONV0 + 2 * CONV_W
FOX_Q0 = GMLP0 + 2 * GMLP_W
FOX_K0 = FOX_Q0 + FOX_W
FOX_V0 = FOX_K0 + FOX_W
FOX_F0 = FOX_V0 + FOX_W
IN_COLS = FOX_F0 + FOX_HEADS

kernel_name = "hybrid_parallel_group_decoder"


def _rms_norm(x, g):
    xf = x.astype(jnp.float32)
    y = xf * lax.rsqrt(jnp.mean(xf * xf, axis=-1, keepdims=True) + EPS)
    return (y * g.astype(jnp.float32)).astype(x.dtype)


def _layer_norm(x, g, b):
    xf = x.astype(jnp.float32)
    mu = jnp.mean(xf, axis=-1, keepdims=True)
    var = jnp.mean(jnp.square(xf - mu), axis=-1, keepdims=True)
    y = (xf - mu) * lax.rsqrt(var + EPS)
    return (y * g.astype(jnp.float32) + b.astype(jnp.float32)).astype(x.dtype)


def _modulate(xn, shift, scale):
    return xn * (1 + scale[:, None, :]) + shift[:, None, :]


def _swiglu(x, w_in, w_out):
    gate, up = jnp.split(x @ w_in, 2, axis=-1)
    return (jax.nn.silu(gate) * up) @ w_out


def _differential_attention(q, k, v, lam, lam_init, out_g):
    B, S, H, _, dq = q.shape
    dv = v.shape[-1]
    nb = S // Q_BLOCK
    q_blocks = jnp.moveaxis(q.reshape(B, nb, Q_BLOCK, H, 2, dq), 1, 0)
    k_pos = jnp.arange(S)
    scale = dq ** -0.5

    def one_block(args):
        qi, i = args
        s = jnp.einsum('bqhcd,bkhcd->bhcqk', qi, k).astype(jnp.float32) * scale
        q_pos = i * Q_BLOCK + jnp.arange(Q_BLOCK)
        causal = k_pos[None, :] <= q_pos[:, None]
        p = jax.nn.softmax(jnp.where(causal, s, NEG_INF), axis=-1)
        a = p[:, :, 0] - lam * p[:, :, 1]
        return jnp.einsum('bhqk,bkhe->bqhe', a.astype(v.dtype), v)

    o = lax.map(one_block, (q_blocks, jnp.arange(nb)))
    o = jnp.moveaxis(o, 0, 1).reshape(B, S, H, dv)
    o = _rms_norm(o, out_g) * (1.0 - lam_init)
    return o.reshape(B, S, H * dv)


def _conformer_conv(z, w, b, ng, nb_):
    a, g = jnp.split(z, 2, axis=-1)
    y = a * jax.nn.sigmoid(g)
    y = lax.conv_general_dilated(
        y, w[:, None, :].astype(y.dtype), window_strides=(1,),
        padding=[(CONV_KERNEL - 1, 0)],
        dimension_numbers=('NWC', 'WIO', 'NWC'),
        feature_group_count=CONV_W) + b
    return jax.nn.silu(_layer_norm(y, ng, nb_))


def _chunked_spatial_gating(z, ng, nb_, ws, bs):
    u, v = jnp.split(jax.nn.gelu(z), 2, axis=-1)
    v = _layer_norm(v, ng, nb_)
    B, S, _ = v.shape
    nc = S // CHUNK
    vh = v.reshape(B, nc, CHUNK, GMLP_HEADS, HEAD_DIM)
    w = ws * jnp.tril(jnp.ones((CHUNK, CHUNK), ws.dtype))[None]
    s = jnp.einsum('hts,bnshd->bnthd', w, vh) + jnp.transpose(bs)[None, None, :, :, None]
    return u * s.reshape(B, S, GMLP_W)


def _forgetting_attention(q, k, v, log_f):
    B, S, H, d = q.shape
    nb = S // Q_BLOCK
    cum = jnp.cumsum(log_f.astype(jnp.float32), axis=1)
    q_blocks = jnp.moveaxis(q.reshape(B, nb, Q_BLOCK, H, d), 1, 0)
    c_blocks = jnp.moveaxis(cum.reshape(B, nb, Q_BLOCK, H), 1, 0)
    cum_k = jnp.transpose(cum, (0, 2, 1))
    k_pos = jnp.arange(S)
    scale = d ** -0.5

    def one_block(args):
        qi, ci, i = args
        s = jnp.einsum('bqhd,bkhd->bhqk', qi, k).astype(jnp.float32) * scale
        s = s + jnp.transpose(ci, (0, 2, 1))[..., None] - cum_k[:, :, None, :]
        q_pos = i * Q_BLOCK + jnp.arange(Q_BLOCK)
        causal = k_pos[None, :] <= q_pos[:, None]
        p = jax.nn.softmax(jnp.where(causal, s, NEG_INF), axis=-1)
        return jnp.einsum('bhqk,bkhd->bqhd', p.astype(v.dtype), v)

    o = lax.map(one_block, (q_blocks, c_blocks, jnp.arange(nb)))
    return jnp.moveaxis(o, 0, 1).reshape(B, S, H * d)


def setup_inputs(seed: int = 0) -> dict:
    key = jax.random.key(seed)
    ks = jax.random.split(key, 24)
    f32 = jnp.float32

    def nrm(k, shape, s):
        return jax.random.normal(k, shape, f32) * s

    L = DEPTH
    return {
        'x': nrm(ks[0], (BATCH, SEQ, D_MODEL), 1.0),
        'c': nrm(ks[1], (BATCH, D_MODEL), 1.0),
        'ada_w': nrm(ks[2], (L, D_MODEL, N_MOD * D_MODEL), 0.5 * D_MODEL ** -0.5),
        'ada_b': nrm(ks[3], (L, N_MOD * D_MODEL), 0.02),
        'norm_g': 1.0 + nrm(ks[4], (L, 3, D_MODEL), 0.05),
        'ffn1_w_in': nrm(ks[5], (L, D_MODEL, 2 * D_FF), D_MODEL ** -0.5),
        'ffn1_w_out': nrm(ks[6], (L, D_FF, D_MODEL), D_FF ** -0.5),
        'ffn2_w_in': nrm(ks[7], (L, D_MODEL, 2 * D_FF), D_MODEL ** -0.5),
        'ffn2_w_out': nrm(ks[8], (L, D_FF, D_MODEL), D_FF ** -0.5),
        'w_in': nrm(ks[9], (L, D_MODEL, IN_COLS), D_MODEL ** -0.5),
        'w_out': nrm(ks[10], (L, MIX_W, D_MODEL), MIX_W ** -0.5),
        'diff_qk_g': 1.0 + nrm(ks[11], (L, 2, DIFF_QK_DIM), 0.05),
        'diff_lambda': nrm(ks[12], (L, 4, DIFF_QK_DIM), 0.1),
        'diff_out_g': 1.0 + nrm(ks[13], (L, HEAD_DIM), 0.05),
        'conv_w': nrm(ks[14], (L, CONV_KERNEL, CONV_W), CONV_KERNEL ** -0.5),
        'conv_b': nrm(ks[15], (L, CONV_W), 0.02),
        'conv_norm_g': 1.0 + nrm(ks[16], (L, CONV_W), 0.05),
        'conv_norm_b': nrm(ks[17], (L, CONV_W), 0.02),
        'gmlp_norm_g': 1.0 + nrm(ks[18], (L, GMLP_W), 0.05),
        'gmlp_norm_b': nrm(ks[19], (L, GMLP_W), 0.02),
        'gmlp_ws': nrm(ks[20], (L, GMLP_HEADS, CHUNK, CHUNK), CHUNK ** -0.5),
        'gmlp_bs': 1.0 + nrm(ks[21], (L, GMLP_HEADS, CHUNK), 0.1),
        'fox_qk_g': 1.0 + nrm(ks[22], (L, 2, HEAD_DIM), 0.05),
        'fox_fb': 2.0 + nrm(ks[23], (L, FOX_HEADS), 0.5),
    }


def reference(x, c, ada_w, ada_b, norm_g, ffn1_w_in, ffn1_w_out, ffn2_w_in, ffn2_w_out,
              w_in, w_out, diff_qk_g, diff_lambda, diff_out_g, conv_w, conv_b,
              conv_norm_g, conv_norm_b, gmlp_norm_g, gmlp_norm_b, gmlp_ws, gmlp_bs,
              fox_qk_g, fox_fb):
    B, S, _ = x.shape
    h = x
    cond = jax.nn.silu(c)
    for l in range(DEPTH):
        mod = cond @ ada_w[l] + ada_b[l]
        sh1, sc1, g1, sh2, sc2, g2, sh3, sc3, g3 = jnp.split(mod, N_MOD, axis=-1)

        hn = _modulate(_rms_norm(h, norm_g[l, 0]), sh1, sc1)
        h = h + 0.5 * g1[:, None, :] * _swiglu(hn, ffn1_w_in[l], ffn1_w_out[l])

        hn = _modulate(_rms_norm(h, norm_g[l, 1]), sh2, sc2)
        z = hn @ w_in[l]

        qa = _rms_norm(z[..., DIFF_Q0:DIFF_K0].reshape(B, S, DIFF_HEADS, 2, DIFF_QK_DIM), diff_qk_g[l, 0])
        ka = _rms_norm(z[..., DIFF_K0:DIFF_V0].reshape(B, S, DIFF_HEADS, 2, DIFF_QK_DIM), diff_qk_g[l, 1])
        va = z[..., DIFF_V0:CONV0].reshape(B, S, DIFF_HEADS, HEAD_DIM)
        lam_init = 0.8 - 0.6 * math.exp(-0.3 * l)
        lv = diff_lambda[l].astype(jnp.float32)
        lam = jnp.exp(jnp.sum(lv[0] * lv[1])) - jnp.exp(jnp.sum(lv[2] * lv[3])) + lam_init
        o_a = _differential_attention(qa, ka, va, lam, lam_init, diff_out_g[l])

        o_b = _conformer_conv(z[..., CONV0:GMLP0], conv_w[l], conv_b[l], conv_norm_g[l], conv_norm_b[l])

        o_c = _chunked_spatial_gating(z[..., GMLP0:FOX_Q0], gmlp_norm_g[l], gmlp_norm_b[l], gmlp_ws[l], gmlp_bs[l])

        qd = _rms_norm(z[..., FOX_Q0:FOX_K0].reshape(B, S, FOX_HEADS, HEAD_DIM), fox_qk_g[l, 0])
        kd = _rms_norm(z[..., FOX_K0:FOX_V0].reshape(B, S, FOX_HEADS, HEAD_DIM), fox_qk_g[l, 1])
        vd = z[..., FOX_V0:FOX_F0].reshape(B, S, FOX_HEADS, HEAD_DIM)
        log_f = jax.nn.log_sigmoid((z[..., FOX_F0:IN_COLS] + fox_fb[l]).astype(jnp.float32))
        o_d = _forgetting_attention(qd, kd, vd, log_f)

        mixed = jnp.concatenate([o_a, o_b, o_c, o_d], axis=-1)
        h = h + g2[:, None, :] * (mixed @ w_out[l])

        hn = _modulate(_rms_norm(h, norm_g[l, 2]), sh3, sc3)
        h = h + 0.5 * g3[:, None, :] * _swiglu(hn, ffn2_w_in[l], ffn2_w_out[l])
    return h
```

```python
import functools
import math

import numpy as np
import jax
import jax.numpy as jnp
from jax import lax
from jax.experimental import pallas as pl
from jax.experimental.pallas import tpu as pltpu

F32 = jnp.float32
BF16 = jnp.bfloat16

HEAD_DIM = 64
N_HEADS = 4
GROUP_W = N_HEADS * HEAD_DIM
DIFF_QK_DIM = HEAD_DIM // 2
CONV_KERNEL = 31
CHUNK = 128
N_MOD = 9
EPS = 1e-6
NEG_INF = -1e30
LOG2E = math.log2(math.e)

V_ROWS = HEAD_DIM + 16
FOX_PAD = 128
CONV_HALO = 32

TQ = 256
TM_PROJ = 512
TM_FFN = 512
VMEM_LIMIT = 56 * 1024 * 1024


def _dot(a, b):
    return jnp.dot(a, b, preferred_element_type=F32)


def _dot_nt(a, b):
    return lax.dot_general(a, b, (((1,), (1,)), ((), ())), preferred_element_type=F32)


def _sigmoid(x):
    return 1.0 / (1.0 + jnp.exp(-x))


def _silu(x):
    return x * _sigmoid(x)


def _gelu_tanh(x):
    return 0.5 * x * (1.0 + jnp.tanh(math.sqrt(2.0 / math.pi) * (x + 0.044715 * (x * x * x))))


def _log_sigmoid(x):
    return jnp.minimum(x, 0.0) - jnp.log1p(jnp.exp(-jnp.abs(x)))


def _norm_mod(x, g, shift, scale):
    ms = jnp.mean(x * x, axis=-1, keepdims=True)
    y = x * lax.rsqrt(ms + EPS) * g
    return y * (1.0 + scale) + shift


def _layer_norm(x, g, b):
    mu = jnp.mean(x, axis=-1, keepdims=True)
    xc = x - mu
    var = jnp.mean(xc * xc, axis=-1, keepdims=True)
    return xc * lax.rsqrt(var + EPS) * g + b


def _split3(x):
    hi = x.astype(BF16)
    r = x - hi.astype(F32)
    mid = r.astype(BF16)
    lo = (r - mid.astype(F32)).astype(BF16)
    return hi, mid, lo


def _ada_kernel(c_ref, w_ref, b_ref, o_ref):
    c = c_ref[...]
    cond = _silu(c).astype(BF16)
    o_ref[0] = _dot(cond, w_ref[0].astype(BF16)) + b_ref[0]


def _ada_call(c, ada_w, ada_b):
    n_layers, d, nm = ada_w.shape
    b = c.shape[0]
    bp = 16
    tn = nm // 8
    c_pad = jnp.zeros((bp, d), F32).at[:b].set(c)
    out = pl.pallas_call(
        _ada_kernel,
        grid=(n_layers, nm // tn),
        in_specs=[
            pl.BlockSpec((bp, d), lambda l, j: (0, 0)),
            pl.BlockSpec((1, d, tn), lambda l, j: (l, 0, j)),
            pl.BlockSpec((1, 1, tn), lambda l, j: (l, 0, j)),
        ],
        out_specs=pl.BlockSpec((1, bp, tn), lambda l, j: (l, 0, j)),
        out_shape=jax.ShapeDtypeStruct((n_layers, bp, nm), F32),
        compiler_params=pltpu.CompilerParams(
            dimension_semantics=("arbitrary", "arbitrary"), vmem_limit_bytes=VMEM_LIMIT),
        name="ada_mod",
    )(c_pad, ada_w, ada_b.reshape(n_layers, 1, nm))
    return out[:, :b].reshape(n_layers, b, N_MOD, d)


def _ffn_kernel(*refs, with_mix, mod_row, n_steps):
    if with_mix:
        (h_ref, mod_ref, g_ref, oa_ref, ob_ref, oc_ref, od_ref, wmix_ref,
         wg_ref, wu_ref, wo_ref, out_ref, hn_ref, acc_ref, base_ref) = refs
    else:
        (h_ref, mod_ref, g_ref, wg_ref, wu_ref, wo_ref, out_ref, hn_ref, acc_ref) = refs
    j = pl.program_id(1)

    @pl.when(j == 0)
    def _():
        x = h_ref[...]
        if with_mix:
            mix = _dot(oa_ref[...], wmix_ref[0 * GROUP_W:1 * GROUP_W, :])
            mix += _dot(ob_ref[...], wmix_ref[1 * GROUP_W:2 * GROUP_W, :])
            mix += _dot(oc_ref[...], wmix_ref[2 * GROUP_W:3 * GROUP_W, :])
            mix += _dot(od_ref[...], wmix_ref[3 * GROUP_W:4 * GROUP_W, :])
            x = x + mod_ref[0, mod_row - 1:mod_row, :] * mix
            base_ref[...] = x
        hn = _norm_mod(x, g_ref[...], mod_ref[0, mod_row:mod_row + 1, :],
                       mod_ref[0, mod_row + 1:mod_row + 2, :])
        hn_ref[...] = hn.astype(BF16)
        acc_ref[...] = jnp.zeros_like(acc_ref)

    hn = hn_ref[...]
    gate = _dot(hn, wg_ref[...])
    up = _dot(hn, wu_ref[...])
    act = (_silu(gate) * up).astype(BF16)
    acc_ref[...] += _dot(act, wo_ref[...])

    @pl.when(j == n_steps - 1)
    def _():
        base = base_ref[...] if with_mix else h_ref[...]
        out_ref[...] = base + (0.5 * mod_ref[0, mod_row + 2:mod_row + 3, :]) * acc_ref[...]


def _ffn_call(h, mod, g, w_in, w_out, *, seq, mod_row, mix=None):
    n, d = h.shape
    d_ff = w_out.shape[0]
    tm = TM_FFN
    tf = d_ff // 2
    n_steps = d_ff // tf
    tiles_per_seq = seq // tm
    with_mix = mix is not None

    row = lambda i, j: (i, 0)
    in_specs = [
        pl.BlockSpec((tm, d), row),
        pl.BlockSpec((1, N_MOD, d), lambda i, j: (i // tiles_per_seq, 0, 0)),
        pl.BlockSpec((1, d), lambda i, j: (0, 0)),
    ]
    args = [h, mod, g.reshape(1, d)]
    scratch = [pltpu.VMEM((tm, d), BF16), pltpu.VMEM((tm, d), F32)]
    if with_mix:
        o_a, o_b, o_c, o_d, w_mix = mix
        in_specs += [pl.BlockSpec((tm, GROUP_W), row)] * 4
        in_specs += [pl.BlockSpec((d, d), lambda i, j: (0, 0))]
        args += [o_a, o_b, o_c, o_d, w_mix]
        scratch += [pltpu.VMEM((tm, d), F32)]
    in_specs += [
        pl.BlockSpec((d, tf), lambda i, j: (0, j)),
        pl.BlockSpec((d, tf), lambda i, j: (0, j + n_steps)),
        pl.BlockSpec((tf, d), lambda i, j: (j, 0)),
    ]
    args += [w_in, w_in, w_out]
    return pl.pallas_call(
        functools.partial(_ffn_kernel, with_mix=with_mix, mod_row=mod_row, n_steps=n_steps),
        grid=(n // tm, n_steps),
        in_specs=in_specs,
        out_specs=pl.BlockSpec((tm, d), row),
        out_shape=jax.ShapeDtypeStruct((n, d), F32),
        scratch_shapes=scratch,
        compiler_params=pltpu.CompilerParams(
            dimension_semantics=("arbitrary", "arbitrary"), vmem_limit_bytes=VMEM_LIMIT),
        name="ffn_mix" if with_mix else "ffn",
    )(*args)


_R_KA = 0
_R_CONV = _R_KA + GROUP_W
_R_GMLP = _R_CONV + 2 * GROUP_W
_R_KD = _R_GMLP + 2 * GROUP_W
_R_F = _R_KD + N_HEADS * FOX_PAD
_R_COLS = _R_F + 128
_T_QA = 0
_T_VA = _T_QA + GROUP_W
_T_QD = _T_VA + N_HEADS * V_ROWS
_T_VD = _T_QD + N_HEADS * FOX_PAD
_T_ROWS = _T_VD + N_HEADS * V_ROWS


def _proj_kernel(h_ref, mod_ref, g_ref, wrm_ref, wt_ref,
                 bd32_ref, bd128_ref, ltri_ref, ek_ref, eq_ref,
                 gka_ref, gqa_ref, gkd_ref, gqd_ref, onek_ref, oneq_ref, fb_ref,
                 cw_ref, cb_ref, cng_ref, cnb_ref,
                 gng_ref, gnb_ref, gws_ref, gbs_ref,
                 ka_ref, qa_ref, va_ref, kd_ref, qd_ref, vd_ref, ob_ref, oc_ref,
                 ybuf_ref, carry_ref, *, tiles_per_seq):
    tm = h_ref.shape[0]
    i = pl.program_id(0)

    @pl.when(i % tiles_per_seq == 0)
    def _():
        ybuf_ref[0:CONV_HALO, :] = jnp.zeros((CONV_HALO, GROUP_W), F32)
        carry_ref[...] = jnp.zeros_like(carry_ref)

    hn = _norm_mod(h_ref[...], g_ref[...], mod_ref[0, 3:4, :], mod_ref[0, 4:5, :]).astype(BF16)
    zr = _dot(hn, wrm_ref[...])
    zt = _dot_nt(wt_ref[...], hn)

    zk = zr[:, _R_KA:_R_KA + GROUP_W]
    ss = _dot((zk * zk).astype(BF16), bd32_ref[...])
    ka_ref[...] = (zk * lax.rsqrt(ss * (1.0 / DIFF_QK_DIM) + EPS) * gka_ref[...]).astype(BF16)
    zq = zt[_T_QA:_T_QA + GROUP_W, :]
    ss = _dot(bd32_ref[...], (zq * zq).astype(BF16))
    qa = (zq * lax.rsqrt(ss * (1.0 / DIFF_QK_DIM) + EPS) * gqa_ref[...]).astype(BF16)
    vrow = lax.broadcasted_iota(jnp.int32, (N_HEADS * V_ROWS, tm), 0) % V_ROWS
    ones_rows = (vrow >= HEAD_DIM).astype(F32)
    va = (zt[_T_VA:_T_VA + N_HEADS * V_ROWS, :] + ones_rows).astype(BF16)
    vd = (zt[_T_VD:_T_VD + N_HEADS * V_ROWS, :] + ones_rows).astype(BF16)

    lf = _log_sigmoid(zr[:, _R_F:_R_F + 128] + fb_ref[...]) * LOG2E
    ltri = ltri_ref[...]
    hi, mid, lo = _split3(lf)
    cum = _dot(ltri, hi) + _dot(ltri, mid) + _dot(ltri, lo) + carry_ref[0:1, :]
    carry_ref[...] = jnp.broadcast_to(cum[tm - 1:tm, :], carry_ref.shape)
    hi, mid, lo = _split3(cum)
    aug_k = _dot(hi, ek_ref[0]) + _dot(mid, ek_ref[1]) + _dot(lo, ek_ref[2]) + onek_ref[...]
    aug_q = (_dot_nt(eq_ref[0], hi) + _dot_nt(eq_ref[1], mid) + _dot_nt(eq_ref[2], lo)
             + oneq_ref[...])
    zk = zr[:, _R_KD:_R_KD + N_HEADS * FOX_PAD]
    ss = _dot((zk * zk).astype(BF16), bd128_ref[...])
    kd_ref[...] = (zk * lax.rsqrt(ss * (1.0 / HEAD_DIM) + EPS) * gkd_ref[...] + aug_k).astype(BF16)
    zq = zt[_T_QD:_T_QD + N_HEADS * FOX_PAD, :]
    ss = _dot(bd128_ref[...], (zq * zq).astype(BF16))
    qd = (zq * lax.rsqrt(ss * (1.0 / HEAD_DIM) + EPS) * gqd_ref[...] + aug_q).astype(BF16)

    for cb in range(tm // TQ):
        sl = slice(cb * TQ, (cb + 1) * TQ)
        qa_ref[cb] = qa[:, sl]
        va_ref[cb] = va[:, sl]
        qd_ref[cb] = qd[:, sl]
        vd_ref[cb] = vd[:, sl]

    y = zr[:, _R_CONV:_R_CONV + GROUP_W] * _sigmoid(zr[:, _R_CONV + GROUP_W:_R_CONV + 2 * GROUP_W])
    ybuf_ref[CONV_HALO:CONV_HALO + tm, :] = y
    off = CONV_HALO - (CONV_KERNEL - 1)
    conv = jnp.zeros((tm, GROUP_W), F32) + cb_ref[...]
    for t in range(CONV_KERNEL):
        conv += cw_ref[t:t + 1, :] * ybuf_ref[off + t:off + t + tm, :]
    ybuf_ref[0:CONV_HALO, :] = ybuf_ref[tm:tm + CONV_HALO, :]
    ob_ref[...] = _silu(_layer_norm(conv, cng_ref[...], cnb_ref[...])).astype(BF16)

    gz = _gelu_tanh(zr[:, _R_GMLP:_R_GMLP + 2 * GROUP_W])
    u = gz[:, :GROUP_W]
    vn = _layer_norm(gz[:, GROUP_W:], gng_ref[...], gnb_ref[...]).astype(BF16)
    tri = (lax.broadcasted_iota(jnp.int32, (N_HEADS * CHUNK, CHUNK), 0) % CHUNK
           >= lax.broadcasted_iota(jnp.int32, (N_HEADS * CHUNK, CHUNK), 1))
    ws = jnp.where(tri, gws_ref[...], 0.0).astype(BF16)
    lane_head = lax.broadcasted_iota(jnp.int32, (CHUNK, GROUP_W), 1) // HEAD_DIM
    for ci in range(tm // CHUNK):
        rows = slice(ci * CHUNK, (ci + 1) * CHUNK)
        res = _dot(ws, vn[rows, :])
        s = res[0:CHUNK, :]
        for hd in range(1, N_HEADS):
            s = jnp.where(lane_head == hd, res[hd * CHUNK:(hd + 1) * CHUNK, :], s)
        oc_ref[rows, :] = (u[rows, :] * (s + gbs_ref[...])).astype(BF16)


def _proj_constants(tm):
    def block_diag(n, blk):
        idx = np.arange(n) // blk
        return (idx[:, None] == idx[None, :]).astype(np.float32)

    ltri = np.tril(np.ones((tm, tm), np.float32))
    ek = np.zeros((3, 128, N_HEADS * FOX_PAD), np.float32)
    eq = np.zeros((3, N_HEADS * FOX_PAD, 128), np.float32)
    onek = np.zeros((1, N_HEADS * FOX_PAD), np.float32)
    oneq = np.zeros((N_HEADS * FOX_PAD, 1), np.float32)
    for hd in range(N_HEADS):
        base = hd * FOX_PAD + HEAD_DIM
        for p in range(3):
            ek[p, hd, base + p] = -1.0
            eq[p, base + 3 + p, hd] = 1.0
        onek[0, base + 3:base + 6] = 1.0
        oneq[base:base + 3, 0] = 1.0
    return dict(
        bd32=jnp.asarray(block_diag(GROUP_W, DIFF_QK_DIM), BF16),
        bd128=jnp.asarray(block_diag(N_HEADS * FOX_PAD, FOX_PAD), BF16),
        ltri=jnp.asarray(ltri, BF16), ek=jnp.asarray(ek, BF16), eq=jnp.asarray(eq, BF16),
        onek=jnp.asarray(onek), oneq=jnp.asarray(oneq))


def _proj_weights(w_in):
    d = w_in.shape[0]
    c0 = 0
    qa = w_in[:, c0:c0 + GROUP_W]; c0 += GROUP_W
    ka = w_in[:, c0:c0 + GROUP_W]; c0 += GROUP_W
    va = w_in[:, c0:c0 + GROUP_W]; c0 += GROUP_W
    conv = w_in[:, c0:c0 + 2 * GROUP_W]; c0 += 2 * GROUP_W
    gmlp = w_in[:, c0:c0 + 2 * GROUP_W]; c0 += 2 * GROUP_W
    qd = w_in[:, c0:c0 + GROUP_W]; c0 += GROUP_W
    kd = w_in[:, c0:c0 + GROUP_W]; c0 += GROUP_W
    vd = w_in[:, c0:c0 + GROUP_W]; c0 += GROUP_W
    wf = w_in[:, c0:c0 + N_HEADS]

    def pad_heads(w, width):
        w = w.reshape(d, N_HEADS, HEAD_DIM)
        w = jnp.pad(w, ((0, 0), (0, 0), (0, width - HEAD_DIM)))
        return w.reshape(d, N_HEADS * width)

    w_rm = jnp.concatenate(
        [ka, conv, gmlp, pad_heads(kd, FOX_PAD), jnp.pad(wf, ((0, 0), (0, 128 - N_HEADS)))], axis=1)
    w_t = jnp.concatenate(
        [qa, pad_heads(va, V_ROWS), pad_heads(qd, FOX_PAD), pad_heads(vd, V_ROWS)], axis=1).T
    return w_rm.astype(BF16), w_t.astype(BF16)


def _proj_call(h, mod, g, w_in, consts, diff_qk_g, fox_qk_g, fox_fb,
               conv_w, conv_b, conv_ng, conv_nb, gmlp_ng, gmlp_nb, gmlp_ws, gmlp_bs, *, seq):
    n, d = h.shape
    tm = TM_PROJ
    tiles_per_seq = seq // tm
    nblk = tm // TQ
    w_rm, w_t = _proj_weights(w_in)

    diff_scale = DIFF_QK_DIM ** -0.5 * LOG2E
    fox_scale = HEAD_DIM ** -0.5 * LOG2E
    gka = jnp.tile(diff_qk_g[1], 2 * N_HEADS).reshape(1, GROUP_W)
    gqa = (jnp.tile(diff_qk_g[0], 2 * N_HEADS) * diff_scale).reshape(GROUP_W, 1)
    pad = jnp.zeros((FOX_PAD - HEAD_DIM,), F32)
    gkd = jnp.tile(jnp.concatenate([fox_qk_g[1], pad]), N_HEADS).reshape(1, N_HEADS * FOX_PAD)
    gqd = jnp.tile(jnp.concatenate([fox_qk_g[0] * fox_scale, pad]), N_HEADS).reshape(N_HEADS * FOX_PAD, 1)
    fb = jnp.pad(fox_fb, (0, 128 - N_HEADS)).reshape(1, 128)
    cw = jnp.pad(conv_w, ((0, 32 - CONV_KERNEL), (0, 0)))
    gws = gmlp_ws.reshape(N_HEADS * CHUNK, CHUNK)
    gbs = jnp.repeat(gmlp_bs.T, HEAD_DIM, axis=1)

    full = lambda a: pl.BlockSpec(a.shape, lambda i, _nd=a.ndim: (0,) * _nd)
    row = lambda i: (i, 0)
    small = [consts["bd32"], consts["bd128"], consts["ltri"], consts["ek"], consts["eq"],
             gka, gqa, gkd, gqd, consts["onek"], consts["oneq"], fb,
             cw, conv_b.reshape(1, -1), conv_ng.reshape(1, -1), conv_nb.reshape(1, -1),
             gmlp_ng.reshape(1, -1), gmlp_nb.reshape(1, -1), gws, gbs]
    in_specs = [
        pl.BlockSpec((tm, d), row),
        pl.BlockSpec((1, N_MOD, d), lambda i: (i // tiles_per_seq, 0, 0)),
        pl.BlockSpec((1, d), lambda i: (0, 0)),
        full(w_rm), full(w_t),
    ] + [full(a) for a in small]
    blk3 = lambda i: (i, 0, 0)
    out_specs = [
        pl.BlockSpec((tm, GROUP_W), row),
        pl.BlockSpec((nblk, GROUP_W, TQ), blk3),
        pl.BlockSpec((nblk, N_HEADS * V_ROWS, TQ), blk3),
        pl.BlockSpec((tm, N_HEADS * FOX_PAD), row),
        pl.BlockSpec((nblk, N_HEADS * FOX_PAD, TQ), blk3),
        pl.BlockSpec((nblk, N_HEADS * V_ROWS, TQ), blk3),
        pl.BlockSpec((tm, GROUP_W), row),
        pl.BlockSpec((tm, GROUP_W), row),
    ]
    nb = n // TQ
    out_shape = [
        jax.ShapeDtypeStruct((n, GROUP_W), BF16),
        jax.ShapeDtypeStruct((nb, GROUP_W, TQ), BF16),
        jax.ShapeDtypeStruct((nb, N_HEADS * V_ROWS, TQ), BF16),
        jax.ShapeDtypeStruct((n, N_HEADS * FOX_PAD), BF16),
        jax.ShapeDtypeStruct((nb, N_HEADS * FOX_PAD, TQ), BF16),
        jax.ShapeDtypeStruct((nb, N_HEADS * V_ROWS, TQ), BF16),
        jax.ShapeDtypeStruct((n, GROUP_W), BF16),
        jax.ShapeDtypeStruct((n, GROUP_W), BF16),
    ]
    return pl.pallas_call(
        functools.partial(_proj_kernel, tiles_per_seq=tiles_per_seq),
        grid=(n // tm,),
        in_specs=in_specs,
        out_specs=out_specs,
        out_shape=out_shape,
        scratch_shapes=[pltpu.VMEM((CONV_HALO + tm, GROUP_W), F32), pltpu.VMEM((8, 128), F32)],
        compiler_params=pltpu.CompilerParams(
            dimension_semantics=("arbitrary",), vmem_limit_bytes=VMEM_LIMIT),
        name="in_proj",
    )(h, mod, g.reshape(1, d), w_rm, w_t, *small)


def _attend(q_t, k_of, v_of, n_full, diag):
    causal = (lax.broadcasted_iota(jnp.int32, (TQ, TQ), 0)
              <= lax.broadcasted_iota(jnp.int32, (TQ, TQ), 1))

    def step(j, carry, masked):
        m, acc = carry
        s = _dot(k_of(j), q_t)
        if masked:
            s = jnp.where(causal, s, NEG_INF)
        m_new = jnp.maximum(m, jnp.max(s, axis=0, keepdims=True))
        p = jnp.exp2(s - m_new).astype(BF16)
        acc = acc * jnp.exp2(m - m_new) + _dot(v_of(j), p)
        return m_new, acc

    init = (jnp.full((1, TQ), NEG_INF, F32), jnp.zeros((V_ROWS, TQ), F32))
    carry = lax.fori_loop(0, n_full, lambda j, c: step(j, c, False), init)
    _, acc = step(diag, carry, True)
    return acc[0:HEAD_DIM, :] / acc[HEAD_DIM:HEAD_DIM + 1, :]


def _diff_attn_kernel(q_ref, k_ref, v_ref, lam_ref, og_ref, o_ref, ot_ref, *, lam_init):
    i = pl.program_id(1)
    q_t = q_ref[0]
    rows = lax.broadcasted_iota(jnp.int32, (GROUP_W, TQ), 0)
    lv = lam_ref[...]
    lam = (jnp.exp(jnp.sum(lv[0:1, :] * lv[1:2, :], axis=-1, keepdims=True))
           - jnp.exp(jnp.sum(lv[2:3, :] * lv[3:4, :], axis=-1, keepdims=True)) + lam_init)
    for hd in range(N_HEADS):
        v_of = lambda j, hd=hd: v_ref[j, hd * V_ROWS:(hd + 1) * V_ROWS, :]
        outs = []
        for c in range(2):
            lo = hd * HEAD_DIM + c * DIFF_QK_DIM
            qm = jnp.where((rows >= lo) & (rows < lo + DIFF_QK_DIM), q_t, jnp.zeros_like(q_t))
            outs.append(_attend(qm, lambda j: k_ref[j], v_of, i, i))
        o = outs[0] - lam * outs[1]
        ms = jnp.mean(o * o, axis=0, keepdims=True)
        ot_ref[hd * HEAD_DIM:(hd + 1) * HEAD_DIM, :] = o * lax.rsqrt(ms + EPS) * og_ref[...]
    o_ref[...] = ot_ref[...].T.astype(BF16)


def _fox_attn_kernel(q_ref, k_ref, v_ref, o_ref, ot_ref):
    i = pl.program_id(1)
    for hd in range(N_HEADS):
        q_t = q_ref[0, hd * FOX_PAD:(hd + 1) * FOX_PAD, :]
        k_of = lambda j, hd=hd: k_ref[j, :, hd * FOX_PAD:(hd + 1) * FOX_PAD]
        v_of = lambda j, hd=hd: v_ref[j, hd * V_ROWS:(hd + 1) * V_ROWS, :]
        ot_ref[hd * HEAD_DIM:(hd + 1) * HEAD_DIM, :] = _attend(q_t, k_of, v_of, i, i)
    o_ref[...] = ot_ref[...].T.astype(BF16)


def _attn_call(kernel, q_t, k, v_t, extra, *, batch, seq, name):
    nq = seq // TQ
    feat = q_t.shape[1]
    k = k.reshape(batch * nq, TQ, feat)
    in_specs = [
        pl.BlockSpec((1, feat, TQ), lambda b, i: (b * nq + i, 0, 0)),
        pl.BlockSpec((nq, TQ, feat), lambda b, i: (b, 0, 0)),
        pl.BlockSpec((nq, N_HEADS * V_ROWS, TQ), lambda b, i: (b, 0, 0)),
    ] + [pl.BlockSpec(a.shape, lambda b, i, _nd=a.ndim: (0,) * _nd) for a in extra]
    return pl.pallas_call(
        kernel,
        grid=(batch, nq),
        in_specs=in_specs,
        out_specs=pl.BlockSpec((TQ, GROUP_W), lambda b, i: (b * nq + i, 0)),
        out_shape=jax.ShapeDtypeStruct((batch * seq, GROUP_W), BF16),
        scratch_shapes=[pltpu.VMEM((GROUP_W, TQ), F32)],
        compiler_params=pltpu.CompilerParams(
            dimension_semantics=("arbitrary", "arbitrary"), vmem_limit_bytes=VMEM_LIMIT),
        name=name,
    )(q_t, k, v_t, *extra)


def kernel(x, c, ada_w, ada_b, norm_g, ffn1_w_in, ffn1_w_out, ffn2_w_in, ffn2_w_out, w_in, w_out, diff_qk_g, diff_lambda, diff_out_g, conv_w, conv_b, conv_norm_g, conv_norm_b, gmlp_norm_g, gmlp_norm_b, gmlp_ws, gmlp_bs, fox_qk_g, fox_fb):
    batch, seq, d = x.shape
    depth = ada_w.shape[0]
    assert seq % TM_PROJ == 0 and seq % TM_FFN == 0 and TM_PROJ % TQ == 0
    assert w_in.shape[-1] == 10 * GROUP_W + N_HEADS and w_out.shape[1] == 4 * GROUP_W

    mods = _ada_call(c, ada_w, ada_b)
    consts = _proj_constants(TM_PROJ)
    h = x.reshape(batch * seq, d)
    for l in range(depth):
        mod = mods[l]
        h = _ffn_call(h, mod, norm_g[l, 0], ffn1_w_in[l].astype(BF16), ffn1_w_out[l].astype(BF16),
                      seq=seq, mod_row=0)
        ka, qa_t, va_t, kd, qd_t, vd_t, o_b, o_c = _proj_call(
            h, mod, norm_g[l, 1], w_in[l], consts, diff_qk_g[l], fox_qk_g[l], fox_fb[l],
            conv_w[l], conv_b[l], conv_norm_g[l], conv_norm_b[l],
            gmlp_norm_g[l], gmlp_norm_b[l], gmlp_ws[l], gmlp_bs[l], seq=seq)
        lam_init = 0.8 - 0.6 * math.exp(-0.3 * l)
        og = (diff_out_g[l] * (1.0 - lam_init)).reshape(HEAD_DIM, 1)
        o_a = _attn_call(functools.partial(_diff_attn_kernel, lam_init=lam_init),
                         qa_t, ka, va_t, [diff_lambda[l], og], batch=batch, seq=seq, name="diff_attn")
        o_d = _attn_call(_fox_attn_kernel, qd_t, kd, vd_t, [], batch=batch, seq=seq, name="fox_attn")
        h = _ffn_call(h, mod, norm_g[l, 2], ffn2_w_in[l].astype(BF16), ffn2_w_out[l].astype(BF16),
                      seq=seq, mod_row=6, mix=(o_a, o_b, o_c, o_d, w_out[l].astype(BF16)))
    return h.reshape(batch, seq, d)
```

```python
import functools
import math

import numpy as np
import jax
import jax.numpy as jnp
from jax import lax
from jax.experimental import pallas as pl
from jax.experimental.pallas import tpu as pltpu

F32 = jnp.float32
BF16 = jnp.bfloat16

HEAD_DIM = 64
N_HEADS = 4
GROUP_W = N_HEADS * HEAD_DIM
DIFF_QK_DIM = HEAD_DIM // 2
CONV_KERNEL = 31
CHUNK = 128
N_MOD = 9
EPS = 1e-6
NEG_INF = -1e30
LOG2E = math.log2(math.e)

V_ROWS = HEAD_DIM + 16
FOX_PAD = 128
CONV_HALO = 32

TQ = 256
TM_PROJ = 512
TM_FFN = 512
VMEM_LIMIT = 56 * 1024 * 1024


def _dot(a, b):
    return jnp.dot(a, b, preferred_element_type=F32)


def _dot_nt(a, b):
    return lax.dot_general(a, b, (((1,), (1,)), ((), ())), preferred_element_type=F32)


def _sigmoid(x):
    return 1.0 / (1.0 + jnp.exp(-x))


def _silu(x):
    return x * _sigmoid(x)


def _gelu_tanh(x):
    return 0.5 * x * (1.0 + jnp.tanh(math.sqrt(2.0 / math.pi) * (x + 0.044715 * (x * x * x))))


def _log_sigmoid(x):
    return jnp.minimum(x, 0.0) - jnp.log1p(jnp.exp(-jnp.abs(x)))


def _norm_mod(x, g, shift, scale):
    ms = jnp.mean(x * x, axis=-1, keepdims=True)
    y = x * lax.rsqrt(ms + EPS) * g
    return y * (1.0 + scale) + shift


def _layer_norm(x, g, b):
    mu = jnp.mean(x, axis=-1, keepdims=True)
    xc = x - mu
    var = jnp.mean(xc * xc, axis=-1, keepdims=True)
    return xc * lax.rsqrt(var + EPS) * g + b


def _split3(x):
    hi = x.astype(BF16)
    r = x - hi.astype(F32)
    mid = r.astype(BF16)
    lo = (r - mid.astype(F32)).astype(BF16)
    return hi, mid, lo


def _ada_kernel(c_ref, w_ref, b_ref, o_ref):
    c = c_ref[...]
    cond = _silu(c).astype(BF16)
    o_ref[0] = _dot(cond, w_ref[0].astype(BF16)) + b_ref[0]


def _ada_call(c, ada_w, ada_b):
    n_layers, d, nm = ada_w.shape
    b = c.shape[0]
    bp = 16
    tn = nm // 8
    c_pad = jnp.zeros((bp, d), F32).at[:b].set(c)
    out = pl.pallas_call(
        _ada_kernel,
        grid=(n_layers, nm // tn),
        in_specs=[
            pl.BlockSpec((bp, d), lambda l, j: (0, 0)),
            pl.BlockSpec((1, d, tn), lambda l, j: (l, 0, j)),
            pl.BlockSpec((1, 1, tn), lambda l, j: (l, 0, j)),
        ],
        out_specs=pl.BlockSpec((1, bp, tn), lambda l, j: (l, 0, j)),
        out_shape=jax.ShapeDtypeStruct((n_layers, bp, nm), F32),
        compiler_params=pltpu.CompilerParams(
            dimension_semantics=("arbitrary", "arbitrary"), vmem_limit_bytes=VMEM_LIMIT),
        name="ada_mod",
    )(c_pad, ada_w, ada_b.reshape(n_layers, 1, nm))
    return out[:, :b].reshape(n_layers, b, N_MOD, d)


def _ffn_kernel(*refs, with_mix, mod_row, n_steps):
    if with_mix:
        (h_ref, mod_ref, g_ref, oa_ref, ob_ref, oc_ref, od_ref, wmix_ref,
         wg_ref, wu_ref, wo_ref, out_ref, hn_ref, acc_ref, base_ref) = refs
    else:
        (h_ref, mod_ref, g_ref, wg_ref, wu_ref, wo_ref, out_ref, hn_ref, acc_ref) = refs
    j = pl.program_id(1)

    @pl.when(j == 0)
    def _():
        x = h_ref[...]
        if with_mix:
            mix = _dot(oa_ref[...], wmix_ref[0 * GROUP_W:1 * GROUP_W, :])
            mix += _dot(ob_ref[...], wmix_ref[1 * GROUP_W:2 * GROUP_W, :])
            mix += _dot(oc_ref[...], wmix_ref[2 * GROUP_W:3 * GROUP_W, :])
            mix += _dot(od_ref[...], wmix_ref[3 * GROUP_W:4 * GROUP_W, :])
            x = x + mod_ref[0, mod_row - 1:mod_row, :] * mix
            base_ref[...] = x
        hn = _norm_mod(x, g_ref[...], mod_ref[0, mod_row:mod_row + 1, :],
                       mod_ref[0, mod_row + 1:mod_row + 2, :])
        hn_ref[...] = hn.astype(BF16)
        acc_ref[...] = jnp.zeros_like(acc_ref)

    hn = hn_ref[...]
    gate = _dot(hn, wg_ref[...])
    up = _dot(hn, wu_ref[...])
    act = (_silu(gate) * up).astype(BF16)
    acc_ref[...] += _dot(act, wo_ref[...])

    @pl.when(j == n_steps - 1)
    def _():
        base = base_ref[...] if with_mix else h_ref[...]
        out_ref[...] = base + (0.5 * mod_ref[0, mod_row + 2:mod_row + 3, :]) * acc_ref[...]


def _ffn_call(h, mod, g, w_in, w_out, *, seq, mod_row, mix=None):
    n, d = h.shape
    d_ff = w_out.shape[0]
    tm = TM_FFN
    tf = d_ff // 2
    n_steps = d_ff // tf
    tiles_per_seq = seq // tm
    with_mix = mix is not None

    row = lambda i, j: (i, 0)
    in_specs = [
        pl.BlockSpec((tm, d), row),
        pl.BlockSpec((1, N_MOD, d), lambda i, j: (i // tiles_per_seq, 0, 0)),
        pl.BlockSpec((1, d), lambda i, j: (0, 0)),
    ]
    args = [h, mod, g.reshape(1, d)]
    scratch = [pltpu.VMEM((tm, d), BF16), pltpu.VMEM((tm, d), F32)]
    if with_mix:
        o_a, o_b, o_c, o_d, w_mix = mix
        in_specs += [pl.BlockSpec((tm, GROUP_W), row)] * 4
        in_specs += [pl.BlockSpec((d, d), lambda i, j: (0, 0))]
        args += [o_a, o_b, o_c, o_d, w_mix]
        scratch += [pltpu.VMEM((tm, d), F32)]
    in_specs += [
        pl.BlockSpec((d, tf), lambda i, j: (0, j)),
        pl.BlockSpec((d, tf), lambda i, j: (0, j + n_steps)),
        pl.BlockSpec((tf, d), lambda i, j: (j, 0)),
    ]
    args += [w_in, w_in, w_out]
    return pl.pallas_call(
        functools.partial(_ffn_kernel, with_mix=with_mix, mod_row=mod_row, n_steps=n_steps),
        grid=(n // tm, n_steps),
        in_specs=in_specs,
        out_specs=pl.BlockSpec((tm, d), row),
        out_shape=jax.ShapeDtypeStruct((n, d), F32),
        scratch_shapes=scratch,
        compiler_params=pltpu.CompilerParams(
            dimension_semantics=("arbitrary", "arbitrary"), vmem_limit_bytes=VMEM_LIMIT),
        name="ffn_mix" if with_mix else "ffn",
    )(*args)


_R_KA = 0
_R_CONV = _R_KA + GROUP_W
_R_GMLP = _R_CONV + 2 * GROUP_W
_R_KD = _R_GMLP + 2 * GROUP_W
_R_F = _R_KD + N_HEADS * FOX_PAD
_R_COLS = _R_F + 128
_T_QA = 0
_T_VA = _T_QA + GROUP_W
_T_QD = _T_VA + N_HEADS * V_ROWS
_T_VD = _T_QD + N_HEADS * FOX_PAD
_T_ROWS = _T_VD + N_HEADS * V_ROWS


def _proj_kernel(h_ref, mod_ref, g_ref, wrm_ref, wt_ref,
                 bd32_ref, bd128_ref, ltri_ref, ek_ref, eq_ref,
                 gka_ref, gqa_ref, gkd_ref, gqd_ref, onek_ref, oneq_ref, fb_ref,
                 cw_ref, cb_ref, cng_ref, cnb_ref,
                 gng_ref, gnb_ref, gws_ref, gbs_ref,
                 ka_ref, qa_ref, va_ref, kd_ref, qd_ref, vd_ref, ob_ref, oc_ref,
                 ybuf_ref, carry_ref, *, tiles_per_seq):
    tm = h_ref.shape[0]
    i = pl.program_id(0)

    @pl.when(i % tiles_per_seq == 0)
    def _():
        ybuf_ref[0:CONV_HALO, :] = jnp.zeros((CONV_HALO, GROUP_W), F32)
        carry_ref[...] = jnp.zeros_like(carry_ref)

    hn = _norm_mod(h_ref[...], g_ref[...], mod_ref[0, 3:4, :], mod_ref[0, 4:5, :]).astype(BF16)
    zr = _dot(hn, wrm_ref[...])
    zt = _dot_nt(wt_ref[...], hn)

    zk = zr[:, _R_KA:_R_KA + GROUP_W]
    ss = _dot((zk * zk).astype(BF16), bd32_ref[...])
    ka_ref[...] = (zk * lax.rsqrt(ss * (1.0 / DIFF_QK_DIM) + EPS) * gka_ref[...]).astype(BF16)
    zq = zt[_T_QA:_T_QA + GROUP_W, :]
    ss = _dot(bd32_ref[...], (zq * zq).astype(BF16))
    qa = (zq * lax.rsqrt(ss * (1.0 / DIFF_QK_DIM) + EPS) * gqa_ref[...]).astype(BF16)
    vrow = lax.broadcasted_iota(jnp.int32, (N_HEADS * V_ROWS, tm), 0) % V_ROWS
    ones_rows = (vrow >= HEAD_DIM).astype(F32)
    va = (zt[_T_VA:_T_VA + N_HEADS * V_ROWS, :] + ones_rows).astype(BF16)
    vd = (zt[_T_VD:_T_VD + N_HEADS * V_ROWS, :] + ones_rows).astype(BF16)

    lf = _log_sigmoid(zr[:, _R_F:_R_F + 128] + fb_ref[...]) * LOG2E
    ltri = ltri_ref[...]
    hi, mid, lo = _split3(lf)
    cum = _dot(ltri, hi) + _dot(ltri, mid) + _dot(ltri, lo) + carry_ref[0:1, :]
    carry_ref[...] = jnp.broadcast_to(cum[tm - 1:tm, :], carry_ref.shape)
    hi, mid, lo = _split3(cum)
    aug_k = _dot(hi, ek_ref[0]) + _dot(mid, ek_ref[1]) + _dot(lo, ek_ref[2]) + onek_ref[...]
    aug_q = (_dot_nt(eq_ref[0], hi) + _dot_nt(eq_ref[1], mid) + _dot_nt(eq_ref[2], lo)
             + oneq_ref[...])
    zk = zr[:, _R_KD:_R_KD + N_HEADS * FOX_PAD]
    ss = _dot((zk * zk).astype(BF16), bd128_ref[...])
    kd_ref[...] = (zk * lax.rsqrt(ss * (1.0 / HEAD_DIM) + EPS) * gkd_ref[...] + aug_k).astype(BF16)
    zq = zt[_T_QD:_T_QD + N_HEADS * FOX_PAD, :]
    ss = _dot(bd128_ref[...], (zq * zq).astype(BF16))
    qd = (zq * lax.rsqrt(ss * (1.0 / HEAD_DIM) + EPS) * gqd_ref[...] + aug_q).astype(BF16)

    for cb in range(tm // TQ):
        sl = slice(cb * TQ, (cb + 1) * TQ)
        qa_ref[cb] = qa[:, sl]
        va_ref[cb] = va[:, sl]
        qd_ref[cb] = qd[:, sl]
        vd_ref[cb] = vd[:, sl]

    y = zr[:, _R_CONV:_R_CONV + GROUP_W] * _sigmoid(zr[:, _R_CONV + GROUP_W:_R_CONV + 2 * GROUP_W])
    ybuf_ref[CONV_HALO:CONV_HALO + tm, :] = y
    off = CONV_HALO - (CONV_KERNEL - 1)
    conv = jnp.zeros((tm, GROUP_W), F32) + cb_ref[...]
    for t in range(CONV_KERNEL):
        conv += cw_ref[t:t + 1, :] * ybuf_ref[off + t:off + t + tm, :]
    ybuf_ref[0:CONV_HALO, :] = ybuf_ref[tm:tm + CONV_HALO, :]
    ob_ref[...] = _silu(_layer_norm(conv, cng_ref[...], cnb_ref[...])).astype(BF16)

    gz = _gelu_tanh(zr[:, _R_GMLP:_R_GMLP + 2 * GROUP_W])
    u = gz[:, :GROUP_W]
    vn = _layer_norm(gz[:, GROUP_W:], gng_ref[...], gnb_ref[...]).astype(BF16)
    tri = (lax.broadcasted_iota(jnp.int32, (N_HEADS * CHUNK, CHUNK), 0) % CHUNK
           >= lax.broadcasted_iota(jnp.int32, (N_HEADS * CHUNK, CHUNK), 1))
    ws = jnp.where(tri, gws_ref[...], 0.0).astype(BF16)
    lane_head = lax.broadcasted_iota(jnp.int32, (CHUNK, GROUP_W), 1) // HEAD_DIM
    for ci in range(tm // CHUNK):
        rows = slice(ci * CHUNK, (ci + 1) * CHUNK)
        res = _dot(ws, vn[rows, :])
        s = res[0:CHUNK, :]
        for hd in range(1, N_HEADS):
            s = jnp.where(lane_head == hd, res[hd * CHUNK:(hd + 1) * CHUNK, :], s)
        oc_ref[rows, :] = (u[rows, :] * (s + gbs_ref[...])).astype(BF16)


def _proj_constants(tm):
    def block_diag(n, blk):
        idx = np.arange(n) // blk
        return (idx[:, None] == idx[None, :]).astype(np.float32)

    ltri = np.tril(np.ones((tm, tm), np.float32))
    ek = np.zeros((3, 128, N_HEADS * FOX_PAD), np.float32)
    eq = np.zeros((3, N_HEADS * FOX_PAD, 128), np.float32)
    onek = np.zeros((1, N_HEADS * FOX_PAD), np.float32)
    oneq = np.zeros((N_HEADS * FOX_PAD, 1), np.float32)
    for hd in range(N_HEADS):
        base = hd * FOX_PAD + HEAD_DIM
        for p in range(3):
            ek[p, hd, base + p] = -1.0
            eq[p, base + 3 + p, hd] = 1.0
        onek[0, base + 3:base + 6] = 1.0
        oneq[base:base + 3, 0] = 1.0
    return dict(
        bd32=jnp.asarray(block_diag(GROUP_W, DIFF_QK_DIM), BF16),
        bd128=jnp.asarray(block_diag(N_HEADS * FOX_PAD, FOX_PAD), BF16),
        ltri=jnp.asarray(ltri, BF16), ek=jnp.asarray(ek, BF16), eq=jnp.asarray(eq, BF16),
        onek=jnp.asarray(onek), oneq=jnp.asarray(oneq))


def _proj_weights(w_in):
    d = w_in.shape[0]
    c0 = 0
    qa = w_in[:, c0:c0 + GROUP_W]; c0 += GROUP_W
    ka = w_in[:, c0:c0 + GROUP_W]; c0 += GROUP_W
    va = w_in[:, c0:c0 + GROUP_W]; c0 += GROUP_W
    conv = w_in[:, c0:c0 + 2 * GROUP_W]; c0 += 2 * GROUP_W
    gmlp = w_in[:, c0:c0 + 2 * GROUP_W]; c0 += 2 * GROUP_W
    qd = w_in[:, c0:c0 + GROUP_W]; c0 += GROUP_W
    kd = w_in[:, c0:c0 + GROUP_W]; c0 += GROUP_W
    vd = w_in[:, c0:c0 + GROUP_W]; c0 += GROUP_W
    wf = w_in[:, c0:c0 + N_HEADS]

    def pad_heads(w, width):
        w = w.reshape(d, N_HEADS, HEAD_DIM)
        w = jnp.pad(w, ((0, 0), (0, 0), (0, width - HEAD_DIM)))
        return w.reshape(d, N_HEADS * width)

    w_rm = jnp.concatenate(
        [ka, conv, gmlp, pad_heads(kd, FOX_PAD), jnp.pad(wf, ((0, 0), (0, 128 - N_HEADS)))], axis=1)
    w_t = jnp.concatenate(
        [qa, pad_heads(va, V_ROWS), pad_heads(qd, FOX_PAD), pad_heads(vd, V_ROWS)], axis=1).T
    return w_rm.astype(BF16), w_t.astype(BF16)


def _proj_call(h, mod, g, w_in, consts, diff_qk_g, fox_qk_g, fox_fb,
               conv_w, conv_b, conv_ng, conv_nb, gmlp_ng, gmlp_nb, gmlp_ws, gmlp_bs, *, seq):
    n, d = h.shape
    tm = TM_PROJ
    tiles_per_seq = seq // tm
    nblk = tm // TQ
    w_rm, w_t = _proj_weights(w_in)

    diff_scale = DIFF_QK_DIM ** -0.5 * LOG2E
    fox_scale = HEAD_DIM ** -0.5 * LOG2E
    gka = jnp.tile(diff_qk_g[1], 2 * N_HEADS).reshape(1, GROUP_W)
    gqa = (jnp.tile(diff_qk_g[0], 2 * N_HEADS) * diff_scale).reshape(GROUP_W, 1)
    pad = jnp.zeros((FOX_PAD - HEAD_DIM,), F32)
    gkd = jnp.tile(jnp.concatenate([fox_qk_g[1], pad]), N_HEADS).reshape(1, N_HEADS * FOX_PAD)
    gqd = jnp.tile(jnp.concatenate([fox_qk_g[0] * fox_scale, pad]), N_HEADS).reshape(N_HEADS * FOX_PAD, 1)
    fb = jnp.pad(fox_fb, (0, 128 - N_HEADS)).reshape(1, 128)
    cw = jnp.pad(conv_w, ((0, 32 - CONV_KERNEL), (0, 0)))
    gws = gmlp_ws.reshape(N_HEADS * CHUNK, CHUNK)
    gbs = jnp.repeat(gmlp_bs.T, HEAD_DIM, axis=1)

    full = lambda a: pl.BlockSpec(a.shape, lambda i, _nd=a.ndim: (0,) * _nd)
    row = lambda i: (i, 0)
    small = [consts["bd32"], consts["bd128"], consts["ltri"], consts["ek"], consts["eq"],
             gka, gqa, gkd, gqd, consts["onek"], consts["oneq"], fb,
             cw, conv_b.reshape(1, -1), conv_ng.reshape(1, -1), conv_nb.reshape(1, -1),
             gmlp_ng.reshape(1, -1), gmlp_nb.reshape(1, -1), gws, gbs]
    in_specs = [
        pl.BlockSpec((tm, d), row),
        pl.BlockSpec((1, N_MOD, d), lambda i: (i // tiles_per_seq, 0, 0)),
        pl.BlockSpec((1, d), lambda i: (0, 0)),
        full(w_rm), full(w_t),
    ] + [full(a) for a in small]
    blk3 = lambda i: (i, 0, 0)
    out_specs = [
        pl.BlockSpec((tm, GROUP_W), row),
        pl.BlockSpec((nblk, GROUP_W, TQ), blk3),
        pl.BlockSpec((nblk, N_HEADS * V_ROWS, TQ), blk3),
        pl.BlockSpec((tm, N_HEADS * FOX_PAD), row),
        pl.BlockSpec((nblk, N_HEADS * FOX_PAD, TQ), blk3),
        pl.BlockSpec((nblk, N_HEADS * V_ROWS, TQ), blk3),
        pl.BlockSpec((tm, GROUP_W), row),
        pl.BlockSpec((tm, GROUP_W), row),
    ]
    nb = n // TQ
    out_shape = [
        jax.ShapeDtypeStruct((n, GROUP_W), BF16),
        jax.ShapeDtypeStruct((nb, GROUP_W, TQ), BF16),
        jax.ShapeDtypeStruct((nb, N_HEADS * V_ROWS, TQ), BF16),
        jax.ShapeDtypeStruct((n, N_HEADS * FOX_PAD), BF16),
        jax.ShapeDtypeStruct((nb, N_HEADS * FOX_PAD, TQ), BF16),
        jax.ShapeDtypeStruct((nb, N_HEADS * V_ROWS, TQ), BF16),
        jax.ShapeDtypeStruct((n, GROUP_W), BF16),
        jax.ShapeDtypeStruct((n, GROUP_W), BF16),
    ]
    return pl.pallas_call(
        functools.partial(_proj_kernel, tiles_per_seq=tiles_per_seq),
        grid=(n // tm,),
        in_specs=in_specs,
        out_specs=out_specs,
        out_shape=out_shape,
        scratch_shapes=[pltpu.VMEM((CONV_HALO + tm, GROUP_W), F32), pltpu.VMEM((8, 128), F32)],
        compiler_params=pltpu.CompilerParams(
            dimension_semantics=("arbitrary",), vmem_limit_bytes=VMEM_LIMIT),
        name="in_proj",
    )(h, mod, g.reshape(1, d), w_rm, w_t, *small)


def _attend(n_chains, q_of, k_of, v_of, s_refs, m_ref, acc_ref, i):
    key_pos = lax.broadcasted_iota(jnp.int32, (TQ, TQ), 0)
    qry_pos = lax.broadcasted_iota(jnp.int32, (TQ, TQ), 1)
    m_ref[...] = jnp.full(m_ref.shape, NEG_INF, F32)
    acc_ref[...] = jnp.zeros(acc_ref.shape, F32)

    def scores(j, s_ref, ch):
        s_ref[ch] = _dot(k_of(ch, j), q_of(ch))

    def update(j, s_ref, ch, masked):
        s = s_ref[ch]
        if masked:
            s = jnp.where(key_pos + (j - i) * TQ <= qry_pos, s, NEG_INF)
        m_old = m_ref[ch]
        m_new = jnp.maximum(m_old, jnp.max(s, axis=0, keepdims=True))
        p = jnp.exp2(s - m_new).astype(BF16)
        acc_ref[ch] = acc_ref[ch] * jnp.exp2(m_old - m_new) + _dot(v_of(ch, j), p)
        m_ref[ch] = m_new

    def step(j_next, s_next, j_cur, s_cur, masked):
        for ch in range(n_chains):
            if j_next is not None:
                scores(j_next, s_next, ch)
            update(j_cur, s_cur, ch, masked)

    n_pairs = i // 2
    for ch in range(n_chains):
        scores(0, s_refs[0], ch)

    def body(jj, carry):
        step(2 * jj + 1, s_refs[1], 2 * jj, s_refs[0], False)
        step(2 * jj + 2, s_refs[0], 2 * jj + 1, s_refs[1], False)
        return carry

    lax.fori_loop(0, n_pairs, body, 0)
    step(2 * n_pairs + 1, s_refs[1], 2 * n_pairs, s_refs[0], True)
    step(None, None, 2 * n_pairs + 1, s_refs[1], True)


def _normalised(acc_ref, ch):
    return acc_ref[ch, 0:HEAD_DIM, :] / acc_ref[ch, HEAD_DIM:HEAD_DIM + 1, :]


def _diff_attn_kernel(q_ref, k_ref, v_ref, lam_ref, og_ref, o_ref,
                      qm_ref, s0_ref, s1_ref, m_ref, acc_ref, ot_ref, *, lam_init):
    i = pl.program_id(1)
    q_t = q_ref[0]
    rows = lax.broadcasted_iota(jnp.int32, (GROUP_W, TQ), 0)
    for ch in range(2 * N_HEADS):
        lo = ch * DIFF_QK_DIM
        qm_ref[ch] = jnp.where((rows >= lo) & (rows < lo + DIFF_QK_DIM), q_t, jnp.zeros_like(q_t))
    _attend(2 * N_HEADS, lambda ch: qm_ref[ch], lambda ch, j: k_ref[j],
            lambda ch, j: v_ref[j, (ch // 2) * V_ROWS:(ch // 2 + 1) * V_ROWS, :],
            (s0_ref, s1_ref), m_ref, acc_ref, i)
    lv = lam_ref[...]
    lam = (jnp.exp(jnp.sum(lv[0:1, :] * lv[1:2, :], axis=-1, keepdims=True))
           - jnp.exp(jnp.sum(lv[2:3, :] * lv[3:4, :], axis=-1, keepdims=True)) + lam_init)
    for hd in range(N_HEADS):
        o = _normalised(acc_ref, 2 * hd) - lam * _normalised(acc_ref, 2 * hd + 1)
        ms = jnp.mean(o * o, axis=0, keepdims=True)
        ot_ref[hd * HEAD_DIM:(hd + 1) * HEAD_DIM, :] = o * lax.rsqrt(ms + EPS) * og_ref[...]
    o_ref[...] = ot_ref[...].T.astype(BF16)


def _fox_attn_kernel(q_ref, k_ref, v_ref, o_ref, s0_ref, s1_ref, m_ref, acc_ref, ot_ref):
    i = pl.program_id(1)
    _attend(N_HEADS, lambda ch: q_ref[0, ch * FOX_PAD:(ch + 1) * FOX_PAD, :],
            lambda ch, j: k_ref[j, :, ch * FOX_PAD:(ch + 1) * FOX_PAD],
            lambda ch, j: v_ref[j, ch * V_ROWS:(ch + 1) * V_ROWS, :],
            (s0_ref, s1_ref), m_ref, acc_ref, i)
    for hd in range(N_HEADS):
        ot_ref[hd * HEAD_DIM:(hd + 1) * HEAD_DIM, :] = _normalised(acc_ref, hd)
    o_ref[...] = ot_ref[...].T.astype(BF16)


def _attn_call(kernel, q_t, k, v_t, extra, *, n_chains, masked_q, batch, seq, name):
    nq = seq // TQ
    feat = q_t.shape[1]
    k = k.reshape(batch * nq, TQ, feat)
    scratch = [pltpu.VMEM((n_chains, feat, TQ), BF16)] if masked_q else []
    scratch += [pltpu.VMEM((n_chains, TQ, TQ), F32), pltpu.VMEM((n_chains, TQ, TQ), F32),
                pltpu.VMEM((n_chains, 1, TQ), F32), pltpu.VMEM((n_chains, V_ROWS, TQ), F32),
                pltpu.VMEM((GROUP_W, TQ), F32)]
    in_specs = [
        pl.BlockSpec((1, feat, TQ), lambda b, i: (b * nq + i, 0, 0)),
        pl.BlockSpec((nq, TQ, feat), lambda b, i: (b, 0, 0)),
        pl.BlockSpec((nq, N_HEADS * V_ROWS, TQ), lambda b, i: (b, 0, 0)),
    ] + [pl.BlockSpec(a.shape, lambda b, i, _nd=a.ndim: (0,) * _nd) for a in extra]
    return pl.pallas_call(
        kernel,
        grid=(batch, nq),
        in_specs=in_specs,
        out_specs=pl.BlockSpec((TQ, GROUP_W), lambda b, i: (b * nq + i, 0)),
        out_shape=jax.ShapeDtypeStruct((batch * seq, GROUP_W), BF16),
        scratch_shapes=scratch,
        compiler_params=pltpu.CompilerParams(
            dimension_semantics=("arbitrary", "arbitrary"), vmem_limit_bytes=VMEM_LIMIT),
        name=name,
    )(q_t, k, v_t, *extra)


def kernel(x, c, ada_w, ada_b, norm_g, ffn1_w_in, ffn1_w_out, ffn2_w_in, ffn2_w_out, w_in, w_out, diff_qk_g, diff_lambda, diff_out_g, conv_w, conv_b, conv_norm_g, conv_norm_b, gmlp_norm_g, gmlp_norm_b, gmlp_ws, gmlp_bs, fox_qk_g, fox_fb):
    batch, seq, d = x.shape
    depth = ada_w.shape[0]
    assert seq % TM_PROJ == 0 and seq % TM_FFN == 0 and TM_PROJ % TQ == 0 and seq % (2 * TQ) == 0
    assert w_in.shape[-1] == 10 * GROUP_W + N_HEADS and w_out.shape[1] == 4 * GROUP_W

    mods = _ada_call(c, ada_w, ada_b)
    consts = _proj_constants(TM_PROJ)
    h = x.reshape(batch * seq, d)
    for l in range(depth):
        mod = mods[l]
        h = _ffn_call(h, mod, norm_g[l, 0], ffn1_w_in[l].astype(BF16), ffn1_w_out[l].astype(BF16),
                      seq=seq, mod_row=0)
        ka, qa_t, va_t, kd, qd_t, vd_t, o_b, o_c = _proj_call(
            h, mod, norm_g[l, 1], w_in[l], consts, diff_qk_g[l], fox_qk_g[l], fox_fb[l],
            conv_w[l], conv_b[l], conv_norm_g[l], conv_norm_b[l],
            gmlp_norm_g[l], gmlp_norm_b[l], gmlp_ws[l], gmlp_bs[l], seq=seq)
        lam_init = 0.8 - 0.6 * math.exp(-0.3 * l)
        og = (diff_out_g[l] * (1.0 - lam_init)).reshape(HEAD_DIM, 1)
        o_a = _attn_call(functools.partial(_diff_attn_kernel, lam_init=lam_init),
                         qa_t, ka, va_t, [diff_lambda[l], og], n_chains=2 * N_HEADS, masked_q=True,
                         batch=batch, seq=seq, name="diff_attn")
        o_d = _attn_call(_fox_attn_kernel, qd_t, kd, vd_t, [], n_chains=N_HEADS, masked_q=False,
                         batch=batch, seq=seq, name="fox_attn")
        h = _ffn_call(h, mod, norm_g[l, 2], ffn2_w_in[l].astype(BF16), ffn2_w_out[l].astype(BF16),
                      seq=seq, mod_row=6, mix=(o_a, o_b, o_c, o_d, w_out[l].astype(BF16)))
    return h.reshape(batch, seq, d)
```

```python
import functools
import math

import numpy as np
import jax
import jax.numpy as jnp
from jax import lax
from jax.experimental import pallas as pl
from jax.experimental.pallas import tpu as pltpu

F32 = jnp.float32
BF16 = jnp.bfloat16

HEAD_DIM = 64
N_HEADS = 4
GROUP_W = N_HEADS * HEAD_DIM
DIFF_QK_DIM = HEAD_DIM // 2
CONV_KERNEL = 31
CHUNK = 128
N_MOD = 9
EPS = 1e-6
NEG_INF = -1e30
LOG2E = math.log2(math.e)

V_ROWS = HEAD_DIM + 16
FOX_PAD = 128
CONV_HALO = 32
CONV_TAIL = 16

TQ = 512
TK = TQ // 2
TM_PROJ = 512
TM_FFN = 512
TF_FFN = 256
VMEM_LIMIT = 56 * 1024 * 1024


def _dot(a, b):
    return jnp.dot(a, b, preferred_element_type=F32)


def _dot_nt(a, b):
    return lax.dot_general(a, b, (((1,), (1,)), ((), ())), preferred_element_type=F32)


def _sigmoid(x):
    return 1.0 / (1.0 + jnp.exp(-x))


def _silu(x):
    return x * _sigmoid(x)


def _gelu_tanh(x):
    return 0.5 * x * (1.0 + jnp.tanh(math.sqrt(2.0 / math.pi) * (x + 0.044715 * (x * x * x))))


def _log_sigmoid(x):
    return jnp.minimum(x, 0.0) - jnp.log1p(jnp.exp(-jnp.abs(x)))


def _norm_mod(x, g, shift, scale):
    ms = jnp.mean(x * x, axis=-1, keepdims=True)
    y = x * lax.rsqrt(ms + EPS) * g
    return y * (1.0 + scale) + shift


def _layer_norm(x, g, b):
    mu = jnp.mean(x, axis=-1, keepdims=True)
    xc = x - mu
    var = jnp.mean(xc * xc, axis=-1, keepdims=True)
    return xc * lax.rsqrt(var + EPS) * g + b


def _split3(x):
    hi = x.astype(BF16)
    r = x - hi.astype(F32)
    mid = r.astype(BF16)
    lo = (r - mid.astype(F32)).astype(BF16)
    return hi, mid, lo


def _ada_kernel(c_ref, w_ref, b_ref, o_ref):
    c = c_ref[...]
    cond = _silu(c).astype(BF16)
    o_ref[0] = _dot(cond, w_ref[0].astype(BF16)) + b_ref[0]


def _ada_call(c, ada_w, ada_b):
    n_layers, d, nm = ada_w.shape
    b = c.shape[0]
    bp = 16
    tn = nm // 8
    c_pad = jnp.zeros((bp, d), F32).at[:b].set(c)
    out = pl.pallas_call(
        _ada_kernel,
        grid=(n_layers, nm // tn),
        in_specs=[
            pl.BlockSpec((bp, d), lambda l, j: (0, 0)),
            pl.BlockSpec((1, d, tn), lambda l, j: (l, 0, j)),
            pl.BlockSpec((1, 1, tn), lambda l, j: (l, 0, j)),
        ],
        out_specs=pl.BlockSpec((1, bp, tn), lambda l, j: (l, 0, j)),
        out_shape=jax.ShapeDtypeStruct((n_layers, bp, nm), F32),
        compiler_params=pltpu.CompilerParams(
            dimension_semantics=("arbitrary", "arbitrary"), vmem_limit_bytes=VMEM_LIMIT),
        name="ada_mod",
    )(c_pad, ada_w, ada_b.reshape(n_layers, 1, nm))
    return out[:, :b].reshape(n_layers, b, N_MOD, d)


def _ffn_kernel(*refs, with_mix, mod_row, d_ff):
    if with_mix:
        (h_ref, mod_ref, g_ref, oa_ref, ob_ref, oc_ref, od_ref, wmix_ref,
         win_ref, wout_ref, out_ref, act_ref) = refs
    else:
        (h_ref, mod_ref, g_ref, win_ref, wout_ref, out_ref, act_ref) = refs
    x = h_ref[...]
    if with_mix:
        mix = _dot(oa_ref[...], wmix_ref[0 * GROUP_W:1 * GROUP_W, :])
        mix += _dot(ob_ref[...], wmix_ref[1 * GROUP_W:2 * GROUP_W, :])
        mix += _dot(oc_ref[...], wmix_ref[2 * GROUP_W:3 * GROUP_W, :])
        mix += _dot(od_ref[...], wmix_ref[3 * GROUP_W:4 * GROUP_W, :])
        x = x + mod_ref[0, mod_row - 1:mod_row, :] * mix
    hn = _norm_mod(x, g_ref[...], mod_ref[0, mod_row:mod_row + 1, :],
                   mod_ref[0, mod_row + 1:mod_row + 2, :]).astype(BF16)
    for c in range(d_ff // TF_FFN):
        cols = slice(c * TF_FFN, (c + 1) * TF_FFN)
        gate = _dot(hn, win_ref[:, cols])
        up = _dot(hn, win_ref[:, d_ff + c * TF_FFN:d_ff + (c + 1) * TF_FFN])
        act_ref[:, cols] = (_silu(gate) * up).astype(BF16)
    out_ref[...] = x + (0.5 * mod_ref[0, mod_row + 2:mod_row + 3, :]) * _dot(act_ref[...], wout_ref[...])


def _resident(shape):
    return pl.BlockSpec(shape, lambda i, _nd=len(shape): (0,) * _nd, pipeline_mode=pl.Buffered(1))


def _ffn_call(h, mod, g, w_in, w_out, *, seq, mod_row, mix=None):
    n, d = h.shape
    d_ff = w_out.shape[0]
    tm = TM_FFN
    tiles_per_seq = seq // tm
    with_mix = mix is not None

    row = lambda i: (i, 0)
    in_specs = [
        pl.BlockSpec((tm, d), row),
        pl.BlockSpec((1, N_MOD, d), lambda i: (i // tiles_per_seq, 0, 0)),
        _resident((1, d)),
    ]
    args = [h, mod, g.reshape(1, d)]
    if with_mix:
        o_a, o_b, o_c, o_d, w_mix = mix
        in_specs += [pl.BlockSpec((tm, GROUP_W), row)] * 4 + [_resident((d, d))]
        args += [o_a, o_b, o_c, o_d, w_mix]
    in_specs += [_resident(w_in.shape), _resident(w_out.shape)]
    args += [w_in, w_out]
    return pl.pallas_call(
        functools.partial(_ffn_kernel, with_mix=with_mix, mod_row=mod_row, d_ff=d_ff),
        grid=(n // tm,),
        in_specs=in_specs,
        out_specs=pl.BlockSpec((tm, d), row),
        out_shape=jax.ShapeDtypeStruct((n, d), F32),
        scratch_shapes=[pltpu.VMEM((tm, d_ff), BF16)],
        compiler_params=pltpu.CompilerParams(
            dimension_semantics=("arbitrary",), vmem_limit_bytes=VMEM_LIMIT),
        name="ffn_mix" if with_mix else "ffn",
    )(*args)


_R_KA = 0
_R_CONV = _R_KA + GROUP_W
_R_GMLP = _R_CONV + 2 * GROUP_W
_R_KD = _R_GMLP + 2 * GROUP_W
_R_F = _R_KD + N_HEADS * FOX_PAD
_R_COLS = _R_F + 128
_T_QA = 0
_T_VA = _T_QA + GROUP_W
_T_QD = _T_VA + N_HEADS * V_ROWS
_T_VD = _T_QD + N_HEADS * FOX_PAD
_T_ROWS = _T_VD + N_HEADS * V_ROWS


def _proj_kernel(h_ref, mod_ref, g_ref, wrm_ref, wt_ref,
                 bd32_ref, bd128_ref, ltri_ref, ek_ref, eq_ref,
                 gka_ref, gqa_ref, gkd_ref, gqd_ref, onek_ref, oneq_ref, fb_ref,
                 cw_ref, cb_ref, cng_ref, cnb_ref,
                 gng_ref, gnb_ref, gws_ref, gbs_ref,
                 ka_ref, qa_ref, va_ref, kd_ref, qd_ref, vd_ref, ob_ref, oc_ref,
                 ybuf_ref, carry_ref, *, tiles_per_seq):
    tm = h_ref.shape[0]
    i = pl.program_id(0)

    @pl.when(i % tiles_per_seq == 0)
    def _():
        ybuf_ref[0:CONV_HALO, :] = jnp.zeros((CONV_HALO, GROUP_W), F32)
        ybuf_ref[CONV_HALO + tm:CONV_HALO + tm + CONV_TAIL, :] = jnp.zeros((CONV_TAIL, GROUP_W), F32)
        carry_ref[...] = jnp.zeros_like(carry_ref)

    hn = _norm_mod(h_ref[...], g_ref[...], mod_ref[0, 3:4, :], mod_ref[0, 4:5, :]).astype(BF16)
    zr = _dot(hn, wrm_ref[...])
    zt = _dot_nt(wt_ref[...], hn)

    zk = zr[:, _R_KA:_R_KA + GROUP_W]
    ss = _dot((zk * zk).astype(BF16), bd32_ref[...])
    ka_ref[...] = (zk * lax.rsqrt(ss * (1.0 / DIFF_QK_DIM) + EPS) * gka_ref[...]).astype(BF16)
    zq = zt[_T_QA:_T_QA + GROUP_W, :]
    ss = _dot(bd32_ref[...], (zq * zq).astype(BF16))
    qa = (zq * lax.rsqrt(ss * (1.0 / DIFF_QK_DIM) + EPS) * gqa_ref[...]).astype(BF16)
    vrow = lax.broadcasted_iota(jnp.int32, (N_HEADS * V_ROWS, tm), 0) % V_ROWS
    ones_rows = (vrow >= HEAD_DIM).astype(F32)
    va = (zt[_T_VA:_T_VA + N_HEADS * V_ROWS, :] + ones_rows).astype(BF16)
    vd = (zt[_T_VD:_T_VD + N_HEADS * V_ROWS, :] + ones_rows).astype(BF16)

    lf = _log_sigmoid(zr[:, _R_F:_R_F + 128] + fb_ref[...]) * LOG2E
    ltri = ltri_ref[...]
    hi, mid, lo = _split3(lf)
    cum = _dot(ltri, hi) + _dot(ltri, mid) + _dot(ltri, lo) + carry_ref[0:1, :]
    carry_ref[...] = jnp.broadcast_to(cum[tm - 1:tm, :], carry_ref.shape)
    hi, mid, lo = _split3(cum)
    aug_k = _dot(hi, ek_ref[0]) + _dot(mid, ek_ref[1]) + _dot(lo, ek_ref[2]) + onek_ref[...]
    aug_q = (_dot_nt(eq_ref[0], hi) + _dot_nt(eq_ref[1], mid) + _dot_nt(eq_ref[2], lo)
             + oneq_ref[...])
    zk = zr[:, _R_KD:_R_KD + N_HEADS * FOX_PAD]
    ss = _dot((zk * zk).astype(BF16), bd128_ref[...])
    kd_ref[...] = (zk * lax.rsqrt(ss * (1.0 / HEAD_DIM) + EPS) * gkd_ref[...] + aug_k).astype(BF16)
    zq = zt[_T_QD:_T_QD + N_HEADS * FOX_PAD, :]
    ss = _dot(bd128_ref[...], (zq * zq).astype(BF16))
    qd = (zq * lax.rsqrt(ss * (1.0 / HEAD_DIM) + EPS) * gqd_ref[...] + aug_q).astype(BF16)

    for cb in range(tm // TQ):
        sl = slice(cb * TQ, (cb + 1) * TQ)
        qa_ref[cb] = qa[:, sl]
        qd_ref[cb] = qd[:, sl]
    for cb in range(tm // TK):
        sl = slice(cb * TK, (cb + 1) * TK)
        va_ref[cb] = va[:, sl]
        vd_ref[cb] = vd[:, sl]

    y = zr[:, _R_CONV:_R_CONV + GROUP_W] * _sigmoid(zr[:, _R_CONV + GROUP_W:_R_CONV + 2 * GROUP_W])
    ybuf_ref[CONV_HALO:CONV_HALO + tm, :] = y
    off = CONV_HALO - (CONV_KERNEL - 1)
    conv = jnp.zeros((tm, GROUP_W), F32) + cb_ref[...]
    for r in range(8):
        z = None
        for q in range((off + CONV_KERNEL - 1 - r) // 8 + 1):
            t = 8 * q + r - off
            if t < 0:
                continue
            term = cw_ref[t:t + 1, :] * ybuf_ref[8 * q:8 * q + tm + 8, :]
            z = term if z is None else z + term
        conv += z[r:r + tm, :]
    ybuf_ref[0:CONV_HALO, :] = ybuf_ref[tm:tm + CONV_HALO, :]
    ob_ref[...] = _silu(_layer_norm(conv, cng_ref[...], cnb_ref[...])).astype(BF16)

    gz = _gelu_tanh(zr[:, _R_GMLP:_R_GMLP + 2 * GROUP_W])
    u = gz[:, :GROUP_W]
    vn = _layer_norm(gz[:, GROUP_W:], gng_ref[...], gnb_ref[...]).astype(BF16)
    tri = (lax.broadcasted_iota(jnp.int32, (N_HEADS * CHUNK, CHUNK), 0) % CHUNK
           >= lax.broadcasted_iota(jnp.int32, (N_HEADS * CHUNK, CHUNK), 1))
    ws = jnp.where(tri, gws_ref[...], 0.0).astype(BF16)
    lane_head = lax.broadcasted_iota(jnp.int32, (CHUNK, GROUP_W), 1) // HEAD_DIM
    for ci in range(tm // CHUNK):
        rows = slice(ci * CHUNK, (ci + 1) * CHUNK)
        res = _dot(ws, vn[rows, :])
        s = res[0:CHUNK, :]
        for hd in range(1, N_HEADS):
            s = jnp.where(lane_head == hd, res[hd * CHUNK:(hd + 1) * CHUNK, :], s)
        oc_ref[rows, :] = (u[rows, :] * (s + gbs_ref[...])).astype(BF16)


def _proj_constants(tm):
    def block_diag(n, blk):
        idx = np.arange(n) // blk
        return (idx[:, None] == idx[None, :]).astype(np.float32)

    ltri = np.tril(np.ones((tm, tm), np.float32))
    ek = np.zeros((3, 128, N_HEADS * FOX_PAD), np.float32)
    eq = np.zeros((3, N_HEADS * FOX_PAD, 128), np.float32)
    onek = np.zeros((1, N_HEADS * FOX_PAD), np.float32)
    oneq = np.zeros((N_HEADS * FOX_PAD, 1), np.float32)
    for hd in range(N_HEADS):
        base = hd * FOX_PAD + HEAD_DIM
        for p in range(3):
            ek[p, hd, base + p] = -1.0
            eq[p, base + 3 + p, hd] = 1.0
        onek[0, base + 3:base + 6] = 1.0
        oneq[base:base + 3, 0] = 1.0
    return dict(
        bd32=jnp.asarray(block_diag(GROUP_W, DIFF_QK_DIM), BF16),
        bd128=jnp.asarray(block_diag(N_HEADS * FOX_PAD, FOX_PAD), BF16),
        ltri=jnp.asarray(ltri, BF16), ek=jnp.asarray(ek, BF16), eq=jnp.asarray(eq, BF16),
        onek=jnp.asarray(onek), oneq=jnp.asarray(oneq))


def _proj_weights(w_in):
    d = w_in.shape[0]
    c0 = 0
    qa = w_in[:, c0:c0 + GROUP_W]; c0 += GROUP_W
    ka = w_in[:, c0:c0 + GROUP_W]; c0 += GROUP_W
    va = w_in[:, c0:c0 + GROUP_W]; c0 += GROUP_W
    conv = w_in[:, c0:c0 + 2 * GROUP_W]; c0 += 2 * GROUP_W
    gmlp = w_in[:, c0:c0 + 2 * GROUP_W]; c0 += 2 * GROUP_W
    qd = w_in[:, c0:c0 + GROUP_W]; c0 += GROUP_W
    kd = w_in[:, c0:c0 + GROUP_W]; c0 += GROUP_W
    vd = w_in[:, c0:c0 + GROUP_W]; c0 += GROUP_W
    wf = w_in[:, c0:c0 + N_HEADS]

    def pad_heads(w, width):
        w = w.reshape(d, N_HEADS, HEAD_DIM)
        w = jnp.pad(w, ((0, 0), (0, 0), (0, width - HEAD_DIM)))
        return w.reshape(d, N_HEADS * width)

    w_rm = jnp.concatenate(
        [ka, conv, gmlp, pad_heads(kd, FOX_PAD), jnp.pad(wf, ((0, 0), (0, 128 - N_HEADS)))], axis=1)
    w_t = jnp.concatenate(
        [qa, pad_heads(va, V_ROWS), pad_heads(qd, FOX_PAD), pad_heads(vd, V_ROWS)], axis=1).T
    return w_rm.astype(BF16), w_t.astype(BF16)


def _proj_call(h, mod, g, w_in, consts, diff_qk_g, fox_qk_g, fox_fb,
               conv_w, conv_b, conv_ng, conv_nb, gmlp_ng, gmlp_nb, gmlp_ws, gmlp_bs, *, seq):
    n, d = h.shape
    tm = TM_PROJ
    tiles_per_seq = seq // tm
    nqb, nkb = tm // TQ, tm // TK
    w_rm, w_t = _proj_weights(w_in)

    diff_scale = DIFF_QK_DIM ** -0.5 * LOG2E
    fox_scale = HEAD_DIM ** -0.5 * LOG2E
    gka = jnp.tile(diff_qk_g[1], 2 * N_HEADS).reshape(1, GROUP_W)
    gqa = (jnp.tile(diff_qk_g[0], 2 * N_HEADS) * diff_scale).reshape(GROUP_W, 1)
    pad = jnp.zeros((FOX_PAD - HEAD_DIM,), F32)
    gkd = jnp.tile(jnp.concatenate([fox_qk_g[1], pad]), N_HEADS).reshape(1, N_HEADS * FOX_PAD)
    gqd = jnp.tile(jnp.concatenate([fox_qk_g[0] * fox_scale, pad]), N_HEADS).reshape(N_HEADS * FOX_PAD, 1)
    fb = jnp.pad(fox_fb, (0, 128 - N_HEADS)).reshape(1, 128)
    cw = jnp.pad(conv_w, ((0, 32 - CONV_KERNEL), (0, 0)))
    gws = gmlp_ws.reshape(N_HEADS * CHUNK, CHUNK)
    gbs = jnp.repeat(gmlp_bs.T, HEAD_DIM, axis=1)

    full = lambda a: _resident(a.shape)
    row = lambda i: (i, 0)
    small = [consts["bd32"], consts["bd128"], consts["ltri"], consts["ek"], consts["eq"],
             gka, gqa, gkd, gqd, consts["onek"], consts["oneq"], fb,
             cw, conv_b.reshape(1, -1), conv_ng.reshape(1, -1), conv_nb.reshape(1, -1),
             gmlp_ng.reshape(1, -1), gmlp_nb.reshape(1, -1), gws, gbs]
    in_specs = [
        pl.BlockSpec((tm, d), row),
        pl.BlockSpec((1, N_MOD, d), lambda i: (i // tiles_per_seq, 0, 0)),
        pl.BlockSpec((1, d), lambda i: (0, 0)),
        full(w_rm), full(w_t),
    ] + [full(a) for a in small]
    blk3 = lambda i: (i, 0, 0)
    out_specs = [
        pl.BlockSpec((tm, GROUP_W), row),
        pl.BlockSpec((nqb, GROUP_W, TQ), blk3),
        pl.BlockSpec((nkb, N_HEADS * V_ROWS, TK), blk3),
        pl.BlockSpec((tm, N_HEADS * FOX_PAD), row),
        pl.BlockSpec((nqb, N_HEADS * FOX_PAD, TQ), blk3),
        pl.BlockSpec((nkb, N_HEADS * V_ROWS, TK), blk3),
        pl.BlockSpec((tm, GROUP_W), row),
        pl.BlockSpec((tm, GROUP_W), row),
    ]
    out_shape = [
        jax.ShapeDtypeStruct((n, GROUP_W), BF16),
        jax.ShapeDtypeStruct((n // TQ, GROUP_W, TQ), BF16),
        jax.ShapeDtypeStruct((n // TK, N_HEADS * V_ROWS, TK), BF16),
        jax.ShapeDtypeStruct((n, N_HEADS * FOX_PAD), BF16),
        jax.ShapeDtypeStruct((n // TQ, N_HEADS * FOX_PAD, TQ), BF16),
        jax.ShapeDtypeStruct((n // TK, N_HEADS * V_ROWS, TK), BF16),
        jax.ShapeDtypeStruct((n, GROUP_W), BF16),
        jax.ShapeDtypeStruct((n, GROUP_W), BF16),
    ]
    return pl.pallas_call(
        functools.partial(_proj_kernel, tiles_per_seq=tiles_per_seq),
        grid=(n // tm,),
        in_specs=in_specs,
        out_specs=out_specs,
        out_shape=out_shape,
        scratch_shapes=[pltpu.VMEM((CONV_HALO + tm + CONV_TAIL, GROUP_W), F32),
                        pltpu.VMEM((8, 128), F32)],
        compiler_params=pltpu.CompilerParams(
            dimension_semantics=("arbitrary",), vmem_limit_bytes=VMEM_LIMIT),
        name="in_proj",
    )(h, mod, g.reshape(1, d), w_rm, w_t, *small)


def _attend(n_chains, q_of, k_of, v_of, s_refs, m_ref, acc_ref, i):
    key_pos = lax.broadcasted_iota(jnp.int32, (TK, TQ), 0)
    qry_pos = lax.broadcasted_iota(jnp.int32, (TK, TQ), 1)
    m_ref[...] = jnp.full(m_ref.shape, NEG_INF, F32)
    acc_ref[...] = jnp.zeros(acc_ref.shape, F32)

    def scores(j, s_ref, ch):
        s_ref[ch] = _dot(k_of(ch, j), q_of(ch))

    def update(j, s_ref, ch, masked):
        s = s_ref[ch]
        if masked:
            s = jnp.where(key_pos + (j * TK - i * TQ) <= qry_pos, s, NEG_INF)
        m_old = m_ref[ch]
        m_new = jnp.maximum(m_old, jnp.max(s, axis=0, keepdims=True))
        p = jnp.exp2(s - m_new).astype(BF16)
        acc_ref[ch] = acc_ref[ch] * jnp.exp2(m_old - m_new) + _dot(v_of(ch, j), p)
        m_ref[ch] = m_new

    def step(j_next, s_next, j_cur, s_cur, masked):
        for ch in range(n_chains):
            if j_next is not None:
                scores(j_next, s_next, ch)
            update(j_cur, s_cur, ch, masked)

    n_pairs = i
    for ch in range(n_chains):
        scores(0, s_refs[0], ch)

    def body(jj, carry):
        step(2 * jj + 1, s_refs[1], 2 * jj, s_refs[0], False)
        step(2 * jj + 2, s_refs[0], 2 * jj + 1, s_refs[1], False)
        return carry

    lax.fori_loop(0, n_pairs, body, 0)
    step(2 * n_pairs + 1, s_refs[1], 2 * n_pairs, s_refs[0], True)
    step(None, None, 2 * n_pairs + 1, s_refs[1], True)


def _normalised(acc_ref, ch):
    return acc_ref[ch, 0:HEAD_DIM, :] / acc_ref[ch, HEAD_DIM:HEAD_DIM + 1, :]


def _diff_attn_kernel(q_ref, k_ref, v_ref, lam_ref, og_ref, o_ref,
                      qm_ref, s0_ref, s1_ref, m_ref, acc_ref, ot_ref, *, lam_init):
    i = pl.program_id(1)
    q_t = q_ref[0]
    rows = lax.broadcasted_iota(jnp.int32, (GROUP_W, TQ), 0)
    for ch in range(2 * N_HEADS):
        lo = ch * DIFF_QK_DIM
        qm_ref[ch] = jnp.where((rows >= lo) & (rows < lo + DIFF_QK_DIM), q_t, jnp.zeros_like(q_t))
    _attend(2 * N_HEADS, lambda ch: qm_ref[ch], lambda ch, j: k_ref[j],
            lambda ch, j: v_ref[j, (ch // 2) * V_ROWS:(ch // 2 + 1) * V_ROWS, :],
            (s0_ref, s1_ref), m_ref, acc_ref, i)
    lv = lam_ref[...]
    lam = (jnp.exp(jnp.sum(lv[0:1, :] * lv[1:2, :], axis=-1, keepdims=True))
           - jnp.exp(jnp.sum(lv[2:3, :] * lv[3:4, :], axis=-1, keepdims=True)) + lam_init)
    for hd in range(N_HEADS):
        o = _normalised(acc_ref, 2 * hd) - lam * _normalised(acc_ref, 2 * hd + 1)
        ms = jnp.mean(o * o, axis=0, keepdims=True)
        ot_ref[hd * HEAD_DIM:(hd + 1) * HEAD_DIM, :] = o * lax.rsqrt(ms + EPS) * og_ref[...]
    o_ref[...] = ot_ref[...].T.astype(BF16)


def _fox_attn_kernel(q_ref, k_ref, v_ref, o_ref, s0_ref, s1_ref, m_ref, acc_ref, ot_ref):
    i = pl.program_id(1)
    _attend(N_HEADS, lambda ch: q_ref[0, ch * FOX_PAD:(ch + 1) * FOX_PAD, :],
            lambda ch, j: k_ref[j, :, ch * FOX_PAD:(ch + 1) * FOX_PAD],
            lambda ch, j: v_ref[j, ch * V_ROWS:(ch + 1) * V_ROWS, :],
            (s0_ref, s1_ref), m_ref, acc_ref, i)
    for hd in range(N_HEADS):
        ot_ref[hd * HEAD_DIM:(hd + 1) * HEAD_DIM, :] = _normalised(acc_ref, hd)
    o_ref[...] = ot_ref[...].T.astype(BF16)


def _attn_call(kernel, q_t, k, v_t, extra, *, n_chains, masked_q, batch, seq, name):
    nq, nk = seq // TQ, seq // TK
    feat = q_t.shape[1]
    k = k.reshape(batch * nk, TK, feat)
    scratch = [pltpu.VMEM((n_chains, feat, TQ), BF16)] if masked_q else []
    scratch += [pltpu.VMEM((n_chains, TK, TQ), F32), pltpu.VMEM((n_chains, TK, TQ), F32),
                pltpu.VMEM((n_chains, 1, TQ), F32), pltpu.VMEM((n_chains, V_ROWS, TQ), F32),
                pltpu.VMEM((GROUP_W, TQ), F32)]
    in_specs = [
        pl.BlockSpec((1, feat, TQ), lambda b, i: (b * nq + i, 0, 0)),
        pl.BlockSpec((nk, TK, feat), lambda b, i: (b, 0, 0)),
        pl.BlockSpec((nk, N_HEADS * V_ROWS, TK), lambda b, i: (b, 0, 0)),
    ] + [pl.BlockSpec(a.shape, lambda b, i, _nd=a.ndim: (0,) * _nd) for a in extra]
    return pl.pallas_call(
        kernel,
        grid=(batch, nq),
        in_specs=in_specs,
        out_specs=pl.BlockSpec((TQ, GROUP_W), lambda b, i: (b * nq + i, 0)),
        out_shape=jax.ShapeDtypeStruct((batch * seq, GROUP_W), BF16),
        scratch_shapes=scratch,
        compiler_params=pltpu.CompilerParams(
            dimension_semantics=("arbitrary", "arbitrary"), vmem_limit_bytes=VMEM_LIMIT),
        name=name,
    )(q_t, k, v_t, *extra)


def kernel(x, c, ada_w, ada_b, norm_g, ffn1_w_in, ffn1_w_out, ffn2_w_in, ffn2_w_out, w_in, w_out, diff_qk_g, diff_lambda, diff_out_g, conv_w, conv_b, conv_norm_g, conv_norm_b, gmlp_norm_g, gmlp_norm_b, gmlp_ws, gmlp_bs, fox_qk_g, fox_fb):
    batch, seq, d = x.shape
    depth = ada_w.shape[0]
    assert seq % TM_PROJ == 0 and seq % TM_FFN == 0 and TM_PROJ % TQ == 0
    assert w_in.shape[-1] == 10 * GROUP_W + N_HEADS and w_out.shape[1] == 4 * GROUP_W
    assert ffn1_w_out.shape[1] % TF_FFN == 0

    mods = _ada_call(c, ada_w, ada_b)
    consts = _proj_constants(TM_PROJ)
    h = x.reshape(batch * seq, d)
    for l in range(depth):
        mod = mods[l]
        h = _ffn_call(h, mod, norm_g[l, 0], ffn1_w_in[l].astype(BF16), ffn1_w_out[l].astype(BF16),
                      seq=seq, mod_row=0)
        ka, qa_t, va_t, kd, qd_t, vd_t, o_b, o_c = _proj_call(
            h, mod, norm_g[l, 1], w_in[l], consts, diff_qk_g[l], fox_qk_g[l], fox_fb[l],
            conv_w[l], conv_b[l], conv_norm_g[l], conv_norm_b[l],
            gmlp_norm_g[l], gmlp_norm_b[l], gmlp_ws[l], gmlp_bs[l], seq=seq)
        lam_init = 0.8 - 0.6 * math.exp(-0.3 * l)
        og = (diff_out_g[l] * (1.0 - lam_init)).reshape(HEAD_DIM, 1)
        o_a = _attn_call(functools.partial(_diff_attn_kernel, lam_init=lam_init),
                         qa_t, ka, va_t, [diff_lambda[l], og], n_chains=2 * N_HEADS, masked_q=True,
                         batch=batch, seq=seq, name="diff_attn")
        o_d = _attn_call(_fox_attn_kernel, qd_t, kd, vd_t, [], n_chains=N_HEADS, masked_q=False,
                         batch=batch, seq=seq, name="fox_attn")
        h = _ffn_call(h, mod, norm_g[l, 2], ffn2_w_in[l].astype(BF16), ffn2_w_out[l].astype(BF16),
                      seq=seq, mod_row=6, mix=(o_a, o_b, o_c, o_d, w_out[l].astype(BF16)))
    return h.reshape(batch, seq, d)
```

```python
import functools
import math

import numpy as np
import jax
import jax.numpy as jnp
from jax import lax
from jax.experimental import pallas as pl
from jax.experimental.pallas import tpu as pltpu

F32 = jnp.float32
BF16 = jnp.bfloat16

HEAD_DIM = 64
N_HEADS = 4
GROUP_W = N_HEADS * HEAD_DIM
DIFF_QK_DIM = HEAD_DIM // 2
CONV_KERNEL = 31
CHUNK = 128
N_MOD = 9
EPS = 1e-6
NEG_INF = -1e30
LOG2E = math.log2(math.e)

V_ROWS = HEAD_DIM + 16
FOX_PAD = 128
CONV_HALO = 32
CONV_TAIL = 16

TQ = 512
TK = TQ // 2
TM_PROJ = 512
TM_FFN = 512
TF_FFN = 256
VMEM_LIMIT = 56 * 1024 * 1024


def _dot(a, b):
    return jnp.dot(a, b, preferred_element_type=F32)


def _dot_nt(a, b):
    return lax.dot_general(a, b, (((1,), (1,)), ((), ())), preferred_element_type=F32)


def _sigmoid(x):
    return 1.0 / (1.0 + jnp.exp(-x))


def _silu(x):
    return x * _sigmoid(x)


def _gelu_tanh(x):
    return 0.5 * x * (1.0 + jnp.tanh(math.sqrt(2.0 / math.pi) * (x + 0.044715 * (x * x * x))))


def _log_sigmoid(x):
    return jnp.minimum(x, 0.0) - jnp.log1p(jnp.exp(-jnp.abs(x)))


def _norm_mod(x, g, shift, scale):
    ms = jnp.mean(x * x, axis=-1, keepdims=True)
    y = x * lax.rsqrt(ms + EPS) * g
    return y * (1.0 + scale) + shift


def _layer_norm(x, g, b):
    mu = jnp.mean(x, axis=-1, keepdims=True)
    xc = x - mu
    var = jnp.mean(xc * xc, axis=-1, keepdims=True)
    return xc * lax.rsqrt(var + EPS) * g + b


def _split3(x):
    hi = x.astype(BF16)
    r = x - hi.astype(F32)
    mid = r.astype(BF16)
    lo = (r - mid.astype(F32)).astype(BF16)
    return hi, mid, lo


def _ada_kernel(c_ref, w_ref, b_ref, o_ref):
    c = c_ref[...]
    cond = _silu(c).astype(BF16)
    o_ref[0] = _dot(cond, w_ref[0].astype(BF16)) + b_ref[0]


def _ada_call(c, ada_w, ada_b):
    n_layers, d, nm = ada_w.shape
    b = c.shape[0]
    bp = 16
    tn = nm // 8
    c_pad = jnp.zeros((bp, d), F32).at[:b].set(c)
    out = pl.pallas_call(
        _ada_kernel,
        grid=(n_layers, nm // tn),
        in_specs=[
            pl.BlockSpec((bp, d), lambda l, j: (0, 0)),
            pl.BlockSpec((1, d, tn), lambda l, j: (l, 0, j)),
            pl.BlockSpec((1, 1, tn), lambda l, j: (l, 0, j)),
        ],
        out_specs=pl.BlockSpec((1, bp, tn), lambda l, j: (l, 0, j)),
        out_shape=jax.ShapeDtypeStruct((n_layers, bp, nm), F32),
        compiler_params=pltpu.CompilerParams(
            dimension_semantics=("arbitrary", "arbitrary"), vmem_limit_bytes=VMEM_LIMIT),
        name="ada_mod",
    )(c_pad, ada_w, ada_b.reshape(n_layers, 1, nm))
    return out[:, :b].reshape(n_layers, b, N_MOD, d)


def _ffn_kernel(*refs, with_mix, mod_row, d_ff):
    if with_mix:
        (h_ref, mod_ref, g_ref, oa_ref, ob_ref, oc_ref, od_ref, wmix_ref,
         win_ref, wout_ref, out_ref, act_ref) = refs
    else:
        (h_ref, mod_ref, g_ref, win_ref, wout_ref, out_ref, act_ref) = refs
    x = h_ref[...]
    if with_mix:
        mix = _dot(oa_ref[...], wmix_ref[0 * GROUP_W:1 * GROUP_W, :])
        mix += _dot(ob_ref[...], wmix_ref[1 * GROUP_W:2 * GROUP_W, :])
        mix += _dot(oc_ref[...], wmix_ref[2 * GROUP_W:3 * GROUP_W, :])
        mix += _dot(od_ref[...], wmix_ref[3 * GROUP_W:4 * GROUP_W, :])
        x = x + mod_ref[0, mod_row - 1:mod_row, :] * mix
    hn = _norm_mod(x, g_ref[...], mod_ref[0, mod_row:mod_row + 1, :],
                   mod_ref[0, mod_row + 1:mod_row + 2, :]).astype(BF16)
    for c in range(d_ff // TF_FFN):
        cols = slice(c * TF_FFN, (c + 1) * TF_FFN)
        gate = _dot(hn, win_ref[:, cols])
        up = _dot(hn, win_ref[:, d_ff + c * TF_FFN:d_ff + (c + 1) * TF_FFN])
        act_ref[:, cols] = (_silu(gate) * up).astype(BF16)
    out_ref[...] = x + (0.5 * mod_ref[0, mod_row + 2:mod_row + 3, :]) * _dot(act_ref[...], wout_ref[...])


def _resident(shape):
    return pl.BlockSpec(shape, lambda i, _nd=len(shape): (0,) * _nd, pipeline_mode=pl.Buffered(1))


def _ffn_call(h, mod, g, w_in, w_out, *, seq, mod_row, mix=None):
    n, d = h.shape
    d_ff = w_out.shape[0]
    tm = TM_FFN
    tiles_per_seq = seq // tm
    with_mix = mix is not None

    row = lambda i: (i, 0)
    in_specs = [
        pl.BlockSpec((tm, d), row),
        pl.BlockSpec((1, N_MOD, d), lambda i: (i // tiles_per_seq, 0, 0)),
        _resident((1, d)),
    ]
    args = [h, mod, g.reshape(1, d)]
    if with_mix:
        o_a, o_b, o_c, o_d, w_mix = mix
        in_specs += [pl.BlockSpec((tm, GROUP_W), row)] * 4 + [_resident((d, d))]
        args += [o_a, o_b, o_c, o_d, w_mix]
    in_specs += [_resident(w_in.shape), _resident(w_out.shape)]
    args += [w_in, w_out]
    return pl.pallas_call(
        functools.partial(_ffn_kernel, with_mix=with_mix, mod_row=mod_row, d_ff=d_ff),
        grid=(n // tm,),
        in_specs=in_specs,
        out_specs=pl.BlockSpec((tm, d), row),
        out_shape=jax.ShapeDtypeStruct((n, d), F32),
        scratch_shapes=[pltpu.VMEM((tm, d_ff), BF16)],
        compiler_params=pltpu.CompilerParams(
            dimension_semantics=("arbitrary",), vmem_limit_bytes=VMEM_LIMIT),
        name="ffn_mix" if with_mix else "ffn",
    )(*args)


_R_KA = 0
_R_CONV = _R_KA + GROUP_W
_R_GMLP = _R_CONV + 2 * GROUP_W
_R_KD = _R_GMLP + 2 * GROUP_W
_R_F = _R_KD + N_HEADS * FOX_PAD
_R_COLS = _R_F + 128
_T_QA = 0
_T_VA = _T_QA + GROUP_W
_T_QD = _T_VA + N_HEADS * V_ROWS
_T_VD = _T_QD + N_HEADS * FOX_PAD
_T_ROWS = _T_VD + N_HEADS * V_ROWS


def _proj_kernel(h_ref, mod_ref, g_ref, wrm_ref, wt_ref,
                 bd32_ref, ltri_ref, ek_ref, eq_ref,
                 gka_ref, gqa_ref, gkd_ref, gqd_ref, onek_ref, oneq_ref, fb_ref,
                 cw_ref, cb_ref, cng_ref, cnb_ref,
                 gng_ref, gnb_ref, gws_ref, gbs_ref,
                 ka_ref, qa_ref, va_ref, kd_ref, qd_ref, vd_ref, ob_ref, oc_ref,
                 ybuf_ref, carry_ref, *, tiles_per_seq):
    tm = h_ref.shape[0]
    i = pl.program_id(0)
    sub, r0 = 0, 0

    @pl.when(i % tiles_per_seq == 0)
    def _():
        ybuf_ref[0:CONV_HALO, :] = jnp.zeros((CONV_HALO, GROUP_W), F32)
        ybuf_ref[CONV_HALO + tm:CONV_HALO + tm + CONV_TAIL, :] = jnp.zeros((CONV_TAIL, GROUP_W), F32)
        carry_ref[...] = jnp.zeros_like(carry_ref)

    hn = _norm_mod(h_ref[...], g_ref[...], mod_ref[0, 3:4, :], mod_ref[0, 4:5, :]).astype(BF16)
    zr_all = _dot(hn, wrm_ref[...])
    zt_all = _dot_nt(wt_ref[...], hn)
    zr = lambda col, width: zr_all[:, col:col + width]
    zt = lambda row, height: zt_all[row:row + height, :]
    carry = carry_ref[0:1, :]

    zc = zr(_R_CONV, 2 * GROUP_W)
    y = zc[:, :GROUP_W] * _sigmoid(zc[:, GROUP_W:])
    ybuf_ref[CONV_HALO:CONV_HALO + tm, :] = y
    off = CONV_HALO - (CONV_KERNEL - 1)
    conv = jnp.zeros((tm, GROUP_W), F32) + cb_ref[...]
    for r in range(8):
        z = None
        for q in range((off + CONV_KERNEL - 1 - r) // 8 + 1):
            t = 8 * q + r - off
            if t < 0:
                continue
            term = cw_ref[t:t + 1, :] * ybuf_ref[8 * q:8 * q + tm + 8, :]
            z = term if z is None else z + term
        conv += z[r:r + tm, :]
    ybuf_ref[0:CONV_HALO, :] = ybuf_ref[tm:tm + CONV_HALO, :]
    ob_ref[r0:r0 + tm, :] = _silu(_layer_norm(conv, cng_ref[...], cnb_ref[...])).astype(BF16)

    gz = _gelu_tanh(zr(_R_GMLP, 2 * GROUP_W))
    u = gz[:, :GROUP_W]
    vn = _layer_norm(gz[:, GROUP_W:], gng_ref[...], gnb_ref[...]).astype(BF16)
    tri = (lax.broadcasted_iota(jnp.int32, (N_HEADS * CHUNK, CHUNK), 0) % CHUNK
           >= lax.broadcasted_iota(jnp.int32, (N_HEADS * CHUNK, CHUNK), 1))
    ws = jnp.where(tri, gws_ref[...], 0.0).astype(BF16)
    lane_head = lax.broadcasted_iota(jnp.int32, (CHUNK, GROUP_W), 1) // HEAD_DIM
    for ci in range(tm // CHUNK):
        rows = slice(ci * CHUNK, (ci + 1) * CHUNK)
        res = _dot(ws, vn[rows, :])
        s = res[0:CHUNK, :]
        for hd in range(1, N_HEADS):
            s = jnp.where(lane_head == hd, res[hd * CHUNK:(hd + 1) * CHUNK, :], s)
        oc_ref[r0 + ci * CHUNK:r0 + (ci + 1) * CHUNK, :] = (u[rows, :] * (s + gbs_ref[...])).astype(BF16)

    head_lane = lax.broadcasted_iota(jnp.int32, (tm, 128), 1) < N_HEADS

    def pack3(x):
        hi, mid, lo = _split3(jnp.where(head_lane, x, 0.0))
        return (hi.astype(F32) + pltpu.roll(mid.astype(F32), N_HEADS, axis=1)
                + pltpu.roll(lo.astype(F32), 2 * N_HEADS, axis=1)).astype(BF16)

    lf = _log_sigmoid(zr(_R_F, 128) + fb_ref[...]) * LOG2E
    c3 = _dot(ltri_ref[...], pack3(lf))
    cum = (c3 + pltpu.roll(c3, 128 - N_HEADS, axis=1)
           + pltpu.roll(c3, 128 - 2 * N_HEADS, axis=1)) + carry
    carry_ref[...] = jnp.broadcast_to(cum[tm - 1:tm, :], carry_ref.shape)
    cpk = pack3(cum)
    aug_k = _dot(cpk, ek_ref[...]) + onek_ref[...]
    aug_q = _dot_nt(eq_ref[...], cpk) + oneq_ref[...]
    zk = zr(_R_KD, N_HEADS * FOX_PAD)
    zq = zt(_T_QD, N_HEADS * FOX_PAD)
    for hd in range(N_HEADS):
        hs = slice(hd * FOX_PAD, (hd + 1) * FOX_PAD)
        blk = zk[:, hs]
        ss = jnp.sum(blk * blk, axis=-1, keepdims=True)
        kd_ref[r0:r0 + tm, hs] = (blk * lax.rsqrt(ss * (1.0 / HEAD_DIM) + EPS) * gkd_ref[:, hs]
                                  + aug_k[:, hs]).astype(BF16)
        blk = zq[hs, :]
        ss = jnp.sum(blk * blk, axis=0, keepdims=True)
        qd = (blk * lax.rsqrt(ss * (1.0 / HEAD_DIM) + EPS) * gqd_ref[hs, :] + aug_q[hs, :]).astype(BF16)
        for cb in range(tm // TQ):
            qd_ref[sub * (tm // TQ) + cb, hs, :] = qd[:, cb * TQ:(cb + 1) * TQ]

    zk = zr(_R_KA, GROUP_W)
    ss = _dot((zk * zk).astype(BF16), bd32_ref[...])
    ka_ref[r0:r0 + tm, :] = (zk * lax.rsqrt(ss * (1.0 / DIFF_QK_DIM) + EPS) * gka_ref[...]).astype(BF16)
    zq = zt(_T_QA, GROUP_W)
    ss = _dot(bd32_ref[...], (zq * zq).astype(BF16))
    qa = (zq * lax.rsqrt(ss * (1.0 / DIFF_QK_DIM) + EPS) * gqa_ref[...]).astype(BF16)
    for cb in range(tm // TQ):
        qa_ref[sub * (tm // TQ) + cb] = qa[:, cb * TQ:(cb + 1) * TQ]

    vrow = lax.broadcasted_iota(jnp.int32, (N_HEADS * V_ROWS, tm), 0) % V_ROWS
    ones_rows = (vrow >= HEAD_DIM).astype(F32)
    va = (zt(_T_VA, N_HEADS * V_ROWS) + ones_rows).astype(BF16)
    vd = (zt(_T_VD, N_HEADS * V_ROWS) + ones_rows).astype(BF16)
    for cb in range(tm // TK):
        va_ref[sub * (tm // TK) + cb] = va[:, cb * TK:(cb + 1) * TK]
        vd_ref[sub * (tm // TK) + cb] = vd[:, cb * TK:(cb + 1) * TK]


def _proj_constants(tm):
    def block_diag(n, blk):
        idx = np.arange(n) // blk
        return (idx[:, None] == idx[None, :]).astype(np.float32)

    ltri = np.tril(np.ones((tm, tm), np.float32))
    ek = np.zeros((128, N_HEADS * FOX_PAD), np.float32)
    eq = np.zeros((N_HEADS * FOX_PAD, 128), np.float32)
    onek = np.zeros((1, N_HEADS * FOX_PAD), np.float32)
    oneq = np.zeros((N_HEADS * FOX_PAD, 1), np.float32)
    for hd in range(N_HEADS):
        base = hd * FOX_PAD + HEAD_DIM
        for p in range(3):
            ek[p * N_HEADS + hd, base + p] = -1.0
            eq[base + 3 + p, p * N_HEADS + hd] = 1.0
        onek[0, base + 3:base + 6] = 1.0
        oneq[base:base + 3, 0] = 1.0
    return dict(
        bd32=jnp.asarray(block_diag(GROUP_W, DIFF_QK_DIM), BF16),
        ltri=jnp.asarray(ltri, BF16), ek=jnp.asarray(ek, BF16), eq=jnp.asarray(eq, BF16),
        onek=jnp.asarray(onek), oneq=jnp.asarray(oneq))


def _proj_weights(w_in):
    d = w_in.shape[0]
    c0 = 0
    qa = w_in[:, c0:c0 + GROUP_W]; c0 += GROUP_W
    ka = w_in[:, c0:c0 + GROUP_W]; c0 += GROUP_W
    va = w_in[:, c0:c0 + GROUP_W]; c0 += GROUP_W
    conv = w_in[:, c0:c0 + 2 * GROUP_W]; c0 += 2 * GROUP_W
    gmlp = w_in[:, c0:c0 + 2 * GROUP_W]; c0 += 2 * GROUP_W
    qd = w_in[:, c0:c0 + GROUP_W]; c0 += GROUP_W
    kd = w_in[:, c0:c0 + GROUP_W]; c0 += GROUP_W
    vd = w_in[:, c0:c0 + GROUP_W]; c0 += GROUP_W
    wf = w_in[:, c0:c0 + N_HEADS]

    def pad_heads(w, width):
        w = w.reshape(d, N_HEADS, HEAD_DIM)
        w = jnp.pad(w, ((0, 0), (0, 0), (0, width - HEAD_DIM)))
        return w.reshape(d, N_HEADS * width)

    w_rm = jnp.concatenate(
        [ka, conv, gmlp, pad_heads(kd, FOX_PAD), jnp.pad(wf, ((0, 0), (0, 128 - N_HEADS)))], axis=1)
    w_t = jnp.concatenate(
        [qa, pad_heads(va, V_ROWS), pad_heads(qd, FOX_PAD), pad_heads(vd, V_ROWS)], axis=1).T
    return w_rm.astype(BF16), w_t.astype(BF16)


def _proj_call(h, mod, g, w_in, consts, diff_qk_g, fox_qk_g, fox_fb,
               conv_w, conv_b, conv_ng, conv_nb, gmlp_ng, gmlp_nb, gmlp_ws, gmlp_bs, *, seq):
    n, d = h.shape
    tm = TM_PROJ
    tiles_per_seq = seq // tm
    nqb, nkb = tm // TQ, tm // TK
    w_rm, w_t = _proj_weights(w_in)

    diff_scale = DIFF_QK_DIM ** -0.5 * LOG2E
    fox_scale = HEAD_DIM ** -0.5 * LOG2E
    gka = jnp.tile(diff_qk_g[1], 2 * N_HEADS).reshape(1, GROUP_W)
    gqa = (jnp.tile(diff_qk_g[0], 2 * N_HEADS) * diff_scale).reshape(GROUP_W, 1)
    pad = jnp.zeros((FOX_PAD - HEAD_DIM,), F32)
    gkd = jnp.tile(jnp.concatenate([fox_qk_g[1], pad]), N_HEADS).reshape(1, N_HEADS * FOX_PAD)
    gqd = jnp.tile(jnp.concatenate([fox_qk_g[0] * fox_scale, pad]), N_HEADS).reshape(N_HEADS * FOX_PAD, 1)
    fb = jnp.pad(fox_fb, (0, 128 - N_HEADS)).reshape(1, 128)
    cw = jnp.pad(conv_w, ((0, 32 - CONV_KERNEL), (0, 0)))
    gws = gmlp_ws.reshape(N_HEADS * CHUNK, CHUNK)
    gbs = jnp.repeat(gmlp_bs.T, HEAD_DIM, axis=1)

    full = lambda a: _resident(a.shape)
    row = lambda i: (i, 0)
    small = [consts["bd32"], consts["ltri"], consts["ek"], consts["eq"],
             gka, gqa, gkd, gqd, consts["onek"], consts["oneq"], fb,
             cw, conv_b.reshape(1, -1), conv_ng.reshape(1, -1), conv_nb.reshape(1, -1),
             gmlp_ng.reshape(1, -1), gmlp_nb.reshape(1, -1), gws, gbs]
    in_specs = [
        pl.BlockSpec((tm, d), row),
        pl.BlockSpec((1, N_MOD, d), lambda i: (i // tiles_per_seq, 0, 0)),
        full(g.reshape(1, d)),
        full(w_rm), full(w_t),
    ] + [full(a) for a in small]
    blk3 = lambda i: (i, 0, 0)
    out_specs = [
        pl.BlockSpec((tm, GROUP_W), row),
        pl.BlockSpec((nqb, GROUP_W, TQ), blk3),
        pl.BlockSpec((nkb, N_HEADS * V_ROWS, TK), blk3),
        pl.BlockSpec((tm, N_HEADS * FOX_PAD), row),
        pl.BlockSpec((nqb, N_HEADS * FOX_PAD, TQ), blk3),
        pl.BlockSpec((nkb, N_HEADS * V_ROWS, TK), blk3),
        pl.BlockSpec((tm, GROUP_W), row),
        pl.BlockSpec((tm, GROUP_W), row),
    ]
    out_shape = [
        jax.ShapeDtypeStruct((n, GROUP_W), BF16),
        jax.ShapeDtypeStruct((n // TQ, GROUP_W, TQ), BF16),
        jax.ShapeDtypeStruct((n // TK, N_HEADS * V_ROWS, TK), BF16),
        jax.ShapeDtypeStruct((n, N_HEADS * FOX_PAD), BF16),
        jax.ShapeDtypeStruct((n // TQ, N_HEADS * FOX_PAD, TQ), BF16),
        jax.ShapeDtypeStruct((n // TK, N_HEADS * V_ROWS, TK), BF16),
        jax.ShapeDtypeStruct((n, GROUP_W), BF16),
        jax.ShapeDtypeStruct((n, GROUP_W), BF16),
    ]
    return pl.pallas_call(
        functools.partial(_proj_kernel, tiles_per_seq=tiles_per_seq),
        grid=(n // tm,),
        in_specs=in_specs,
        out_specs=out_specs,
        out_shape=out_shape,
        scratch_shapes=[pltpu.VMEM((CONV_HALO + tm + CONV_TAIL, GROUP_W), F32),
                        pltpu.VMEM((8, 128), F32)],
        compiler_params=pltpu.CompilerParams(
            dimension_semantics=("arbitrary",), vmem_limit_bytes=VMEM_LIMIT),
        name="in_proj",
    )(h, mod, g.reshape(1, d), w_rm, w_t, *small)


def _attend(n_chains, q_of, k_of, v_of, s_refs, m_ref, acc_ref, i):
    key_pos = lax.broadcasted_iota(jnp.int32, (TK, TQ), 0)
    qry_pos = lax.broadcasted_iota(jnp.int32, (TK, TQ), 1)
    m_ref[...] = jnp.full(m_ref.shape, NEG_INF, F32)
    acc_ref[...] = jnp.zeros(acc_ref.shape, F32)

    def scores(j, s_ref, ch):
        s_ref[ch] = _dot(k_of(ch, j), q_of(ch))

    def update(j, s_ref, ch, masked):
        s = s_ref[ch]
        if masked:
            s = jnp.where(key_pos + (j * TK - i * TQ) <= qry_pos, s, NEG_INF)
        m_old = m_ref[ch]
        m_new = jnp.maximum(m_old, jnp.max(s, axis=0, keepdims=True))
        p = jnp.exp2(s - m_new).astype(BF16)
        acc_ref[ch] = acc_ref[ch] * jnp.exp2(m_old - m_new) + _dot(v_of(ch, j), p)
        m_ref[ch] = m_new

    def step(j_next, s_next, j_cur, s_cur, masked):
        for ch in range(n_chains):
            if j_next is not None:
                scores(j_next, s_next, ch)
            update(j_cur, s_cur, ch, masked)

    n_pairs = i
    for ch in range(n_chains):
        scores(0, s_refs[0], ch)

    def body(jj, carry):
        step(2 * jj + 1, s_refs[1], 2 * jj, s_refs[0], False)
        step(2 * jj + 2, s_refs[0], 2 * jj + 1, s_refs[1], False)
        return carry

    lax.fori_loop(0, n_pairs, body, 0)
    step(2 * n_pairs + 1, s_refs[1], 2 * n_pairs, s_refs[0], True)
    step(None, None, 2 * n_pairs + 1, s_refs[1], True)


def _normalised(acc_ref, ch):
    return acc_ref[ch, 0:HEAD_DIM, :] / acc_ref[ch, HEAD_DIM:HEAD_DIM + 1, :]


def _diff_attn_kernel(q_ref, k_ref, v_ref, lam_ref, og_ref, o_ref,
                      qm_ref, s0_ref, s1_ref, m_ref, acc_ref, ot_ref, *, lam_init):
    i = pl.program_id(1)
    q_t = q_ref[0]
    rows = lax.broadcasted_iota(jnp.int32, (GROUP_W, TQ), 0)
    for ch in range(2 * N_HEADS):
        lo = ch * DIFF_QK_DIM
        qm_ref[ch] = jnp.where((rows >= lo) & (rows < lo + DIFF_QK_DIM), q_t, jnp.zeros_like(q_t))
    _attend(2 * N_HEADS, lambda ch: qm_ref[ch], lambda ch, j: k_ref[j],
            lambda ch, j: v_ref[j, (ch // 2) * V_ROWS:(ch // 2 + 1) * V_ROWS, :],
            (s0_ref, s1_ref), m_ref, acc_ref, i)
    lv = lam_ref[...]
    lam = (jnp.exp(jnp.sum(lv[0:1, :] * lv[1:2, :], axis=-1, keepdims=True))
           - jnp.exp(jnp.sum(lv[2:3, :] * lv[3:4, :], axis=-1, keepdims=True)) + lam_init)
    for hd in range(N_HEADS):
        o = _normalised(acc_ref, 2 * hd) - lam * _normalised(acc_ref, 2 * hd + 1)
        ms = jnp.mean(o * o, axis=0, keepdims=True)
        ot_ref[hd * HEAD_DIM:(hd + 1) * HEAD_DIM, :] = o * lax.rsqrt(ms + EPS) * og_ref[...]
    o_ref[...] = ot_ref[...].T.astype(BF16)


def _fox_attn_kernel(q_ref, k_ref, v_ref, o_ref, s0_ref, s1_ref, m_ref, acc_ref, ot_ref):
    i = pl.program_id(1)
    _attend(N_HEADS, lambda ch: q_ref[0, ch * FOX_PAD:(ch + 1) * FOX_PAD, :],
            lambda ch, j: k_ref[j, :, ch * FOX_PAD:(ch + 1) * FOX_PAD],
            lambda ch, j: v_ref[j, ch * V_ROWS:(ch + 1) * V_ROWS, :],
            (s0_ref, s1_ref), m_ref, acc_ref, i)
    for hd in range(N_HEADS):
        ot_ref[hd * HEAD_DIM:(hd + 1) * HEAD_DIM, :] = _normalised(acc_ref, hd)
    o_ref[...] = ot_ref[...].T.astype(BF16)


def _attn_call(kernel, q_t, k, v_t, extra, *, n_chains, masked_q, batch, seq, name):
    nq, nk = seq // TQ, seq // TK
    feat = q_t.shape[1]
    k = k.reshape(batch * nk, TK, feat)
    scratch = [pltpu.VMEM((n_chains, feat, TQ), BF16)] if masked_q else []
    scratch += [pltpu.VMEM((n_chains, TK, TQ), F32), pltpu.VMEM((n_chains, TK, TQ), F32),
                pltpu.VMEM((n_chains, 1, TQ), F32), pltpu.VMEM((n_chains, V_ROWS, TQ), F32),
                pltpu.VMEM((GROUP_W, TQ), F32)]
    in_specs = [
        pl.BlockSpec((1, feat, TQ), lambda b, i: (b * nq + i, 0, 0)),
        pl.BlockSpec((nk, TK, feat), lambda b, i: (b, 0, 0)),
        pl.BlockSpec((nk, N_HEADS * V_ROWS, TK), lambda b, i: (b, 0, 0)),
    ] + [pl.BlockSpec(a.shape, lambda b, i, _nd=a.ndim: (0,) * _nd) for a in extra]
    return pl.pallas_call(
        kernel,
        grid=(batch, nq),
        in_specs=in_specs,
        out_specs=pl.BlockSpec((TQ, GROUP_W), lambda b, i: (b * nq + i, 0)),
        out_shape=jax.ShapeDtypeStruct((batch * seq, GROUP_W), BF16),
        scratch_shapes=scratch,
        compiler_params=pltpu.CompilerParams(
            dimension_semantics=("arbitrary", "arbitrary"), vmem_limit_bytes=VMEM_LIMIT),
        name=name,
    )(q_t, k, v_t, *extra)


def kernel(x, c, ada_w, ada_b, norm_g, ffn1_w_in, ffn1_w_out, ffn2_w_in, ffn2_w_out, w_in, w_out, diff_qk_g, diff_lambda, diff_out_g, conv_w, conv_b, conv_norm_g, conv_norm_b, gmlp_norm_g, gmlp_norm_b, gmlp_ws, gmlp_bs, fox_qk_g, fox_fb):
    batch, seq, d = x.shape
    depth = ada_w.shape[0]
    assert seq % TM_PROJ == 0 and seq % TM_FFN == 0 and TM_PROJ % TQ == 0
    assert w_in.shape[-1] == 10 * GROUP_W + N_HEADS and w_out.shape[1] == 4 * GROUP_W
    assert ffn1_w_out.shape[1] % TF_FFN == 0

    mods = _ada_call(c, ada_w, ada_b)
    consts = _proj_constants(TM_PROJ)
    h = x.reshape(batch * seq, d)
    for l in range(depth):
        mod = mods[l]
        h = _ffn_call(h, mod, norm_g[l, 0], ffn1_w_in[l].astype(BF16), ffn1_w_out[l].astype(BF16),
                      seq=seq, mod_row=0)
        ka, qa_t, va_t, kd, qd_t, vd_t, o_b, o_c = _proj_call(
            h, mod, norm_g[l, 1], w_in[l], consts, diff_qk_g[l], fox_qk_g[l], fox_fb[l],
            conv_w[l], conv_b[l], conv_norm_g[l], conv_norm_b[l],
            gmlp_norm_g[l], gmlp_norm_b[l], gmlp_ws[l], gmlp_bs[l], seq=seq)
        lam_init = 0.8 - 0.6 * math.exp(-0.3 * l)
        og = (diff_out_g[l] * (1.0 - lam_init)).reshape(HEAD_DIM, 1)
        o_a = _attn_call(functools.partial(_diff_attn_kernel, lam_init=lam_init),
                         qa_t, ka, va_t, [diff_lambda[l], og], n_chains=2 * N_HEADS, masked_q=True,
                         batch=batch, seq=seq, name="diff_attn")
        o_d = _attn_call(_fox_attn_kernel, qd_t, kd, vd_t, [], n_chains=N_HEADS, masked_q=False,
                         batch=batch, seq=seq, name="fox_attn")
        h = _ffn_call(h, mod, norm_g[l, 2], ffn2_w_in[l].astype(BF16), ffn2_w_out[l].astype(BF16),
                      seq=seq, mod_row=6, mix=(o_a, o_b, o_c, o_d, w_out[l].astype(BF16)))
    return h.reshape(batch, seq, d)
```

```python
import functools
import math

import numpy as np
import jax
import jax.numpy as jnp
from jax import lax
from jax.experimental import pallas as pl
from jax.experimental.pallas import tpu as pltpu

F32 = jnp.float32
BF16 = jnp.bfloat16

HEAD_DIM = 64
N_HEADS = 4
GROUP_W = N_HEADS * HEAD_DIM
DIFF_QK_DIM = HEAD_DIM // 2
CONV_KERNEL = 31
CHUNK = 128
N_MOD = 9
EPS = 1e-6
NEG_INF = -1e30
LOG2E = math.log2(math.e)

V_ROWS = HEAD_DIM + 16
FOX_PAD = 128
CONV_HALO = 32
CONV_TAIL = 16

TQ = 512
TK = TQ // 2
TM_PROJ = 512
TM_FFN = 512
TF_FFN = 256
VMEM_LIMIT = 56 * 1024 * 1024


def _dot(a, b):
    return jnp.dot(a, b, preferred_element_type=F32)


def _dot_nt(a, b):
    return lax.dot_general(a, b, (((1,), (1,)), ((), ())), preferred_element_type=F32)


def _sigmoid(x):
    return 1.0 / (1.0 + jnp.exp(-x))


def _silu(x):
    return x * _sigmoid(x)


def _gelu_tanh(x):
    return 0.5 * x * (1.0 + jnp.tanh(math.sqrt(2.0 / math.pi) * (x + 0.044715 * (x * x * x))))


def _log_sigmoid(x):
    return jnp.minimum(x, 0.0) - jnp.log1p(jnp.exp(-jnp.abs(x)))


def _norm_mod(x, g, shift, scale):
    ms = jnp.mean(x * x, axis=-1, keepdims=True)
    y = x * lax.rsqrt(ms + EPS) * g
    return y * (1.0 + scale) + shift


def _layer_norm(x, g, b):
    mu = jnp.mean(x, axis=-1, keepdims=True)
    xc = x - mu
    var = jnp.mean(xc * xc, axis=-1, keepdims=True)
    return xc * lax.rsqrt(var + EPS) * g + b


def _split3(x):
    hi = x.astype(BF16)
    r = x - hi.astype(F32)
    mid = r.astype(BF16)
    lo = (r - mid.astype(F32)).astype(BF16)
    return hi, mid, lo


def _ada_kernel(c_ref, w_ref, b_ref, o_ref):
    c = c_ref[...]
    cond = _silu(c).astype(BF16)
    o_ref[0] = _dot(cond, w_ref[0].astype(BF16)) + b_ref[0]


def _ada_call(c, ada_w, ada_b):
    n_layers, d, nm = ada_w.shape
    b = c.shape[0]
    bp = 16
    tn = nm // 8
    c_pad = jnp.zeros((bp, d), F32).at[:b].set(c)
    out = pl.pallas_call(
        _ada_kernel,
        grid=(n_layers, nm // tn),
        in_specs=[
            pl.BlockSpec((bp, d), lambda l, j: (0, 0)),
            pl.BlockSpec((1, d, tn), lambda l, j: (l, 0, j)),
            pl.BlockSpec((1, 1, tn), lambda l, j: (l, 0, j)),
        ],
        out_specs=pl.BlockSpec((1, bp, tn), lambda l, j: (l, 0, j)),
        out_shape=jax.ShapeDtypeStruct((n_layers, bp, nm), F32),
        compiler_params=pltpu.CompilerParams(
            dimension_semantics=("arbitrary", "arbitrary"), vmem_limit_bytes=VMEM_LIMIT),
        name="ada_mod",
    )(c_pad, ada_w, ada_b.reshape(n_layers, 1, nm))
    return out[:, :b].reshape(n_layers, b, N_MOD, d)


def _ffn_kernel(*refs, with_mix, mod_row, d_ff):
    if with_mix:
        (h_ref, mod_ref, g_ref, oa_ref, ob_ref, oc_ref, od_ref, wmix_ref,
         win_ref, wout_ref, out_ref, act_ref) = refs
    else:
        (h_ref, mod_ref, g_ref, win_ref, wout_ref, out_ref, act_ref) = refs
    x = h_ref[...]
    if with_mix:
        mix = _dot(oa_ref[...], wmix_ref[0 * GROUP_W:1 * GROUP_W, :])
        mix += _dot(ob_ref[...], wmix_ref[1 * GROUP_W:2 * GROUP_W, :])
        mix += _dot(oc_ref[...], wmix_ref[2 * GROUP_W:3 * GROUP_W, :])
        mix += _dot(od_ref[...], wmix_ref[3 * GROUP_W:4 * GROUP_W, :])
        x = x + mod_ref[0, mod_row - 1:mod_row, :] * mix
    hn = _norm_mod(x, g_ref[...], mod_ref[0, mod_row:mod_row + 1, :],
                   mod_ref[0, mod_row + 1:mod_row + 2, :]).astype(BF16)
    for c in range(d_ff // TF_FFN):
        cols = slice(c * TF_FFN, (c + 1) * TF_FFN)
        gate = _dot(hn, win_ref[:, cols])
        up = _dot(hn, win_ref[:, d_ff + c * TF_FFN:d_ff + (c + 1) * TF_FFN])
        act_ref[:, cols] = (_silu(gate) * up).astype(BF16)
    out_ref[...] = x + (0.5 * mod_ref[0, mod_row + 2:mod_row + 3, :]) * _dot(act_ref[...], wout_ref[...])


def _resident(shape):
    return pl.BlockSpec(shape, lambda i, _nd=len(shape): (0,) * _nd, pipeline_mode=pl.Buffered(1))


def _ffn_call(h, mod, g, w_in, w_out, *, seq, mod_row, mix=None):
    n, d = h.shape
    d_ff = w_out.shape[0]
    tm = TM_FFN
    tiles_per_seq = seq // tm
    with_mix = mix is not None

    row = lambda i: (i, 0)
    in_specs = [
        pl.BlockSpec((tm, d), row),
        pl.BlockSpec((1, N_MOD, d), lambda i: (i // tiles_per_seq, 0, 0)),
        _resident((1, d)),
    ]
    args = [h, mod, g.reshape(1, d)]
    if with_mix:
        o_a, o_b, o_c, o_d, w_mix = mix
        in_specs += [pl.BlockSpec((tm, GROUP_W), row)] * 4 + [_resident((d, d))]
        args += [o_a, o_b, o_c, o_d, w_mix]
    in_specs += [_resident(w_in.shape), _resident(w_out.shape)]
    args += [w_in, w_out]
    return pl.pallas_call(
        functools.partial(_ffn_kernel, with_mix=with_mix, mod_row=mod_row, d_ff=d_ff),
        grid=(n // tm,),
        in_specs=in_specs,
        out_specs=pl.BlockSpec((tm, d), row),
        out_shape=jax.ShapeDtypeStruct((n, d), F32),
        scratch_shapes=[pltpu.VMEM((tm, d_ff), BF16)],
        compiler_params=pltpu.CompilerParams(
            dimension_semantics=("arbitrary",), vmem_limit_bytes=VMEM_LIMIT),
        name="ffn_mix" if with_mix else "ffn",
    )(*args)


_R_KA = 0
_R_CONV = _R_KA + GROUP_W
_R_GMLP = _R_CONV + 2 * GROUP_W
_R_KD = _R_GMLP + 2 * GROUP_W
_R_F = _R_KD + N_HEADS * FOX_PAD
_R_COLS = _R_F + 128
_T_QA = 0
_T_VA = _T_QA + GROUP_W
_T_QD = _T_VA + N_HEADS * V_ROWS
_T_VD = _T_QD + N_HEADS * FOX_PAD
_T_ROWS = _T_VD + N_HEADS * V_ROWS


def _proj_kernel(h_ref, mod_ref, g_ref, wrm_ref, wt_ref,
                 bd32_ref, ltri_ref, ek_ref, eq_ref,
                 gka_ref, gqa_ref, gkd_ref, gqd_ref, onek_ref, oneq_ref, fb_ref,
                 cw_ref, cb_ref, cng_ref, cnb_ref,
                 gng_ref, gnb_ref, gws_ref, gbs_ref,
                 ka_ref, qa_ref, va_ref, kd_ref, qd_ref, vd_ref, ob_ref, oc_ref,
                 ybuf_ref, carry_ref, *, tiles_per_seq):
    tm = h_ref.shape[0]
    i = pl.program_id(0)
    sub, r0 = 0, 0

    @pl.when(i % tiles_per_seq == 0)
    def _():
        ybuf_ref[0:CONV_HALO, :] = jnp.zeros((CONV_HALO, GROUP_W), F32)
        ybuf_ref[CONV_HALO + tm:CONV_HALO + tm + CONV_TAIL, :] = jnp.zeros((CONV_TAIL, GROUP_W), F32)
        carry_ref[...] = jnp.zeros_like(carry_ref)

    hn = _norm_mod(h_ref[...], g_ref[...], mod_ref[0, 3:4, :], mod_ref[0, 4:5, :]).astype(BF16)
    zr_all = _dot(hn, wrm_ref[...])
    zt_all = _dot_nt(wt_ref[...], hn)
    zr = lambda col, width: zr_all[:, col:col + width]
    zt = lambda row, height: zt_all[row:row + height, :]
    carry = carry_ref[0:1, :]

    zc = zr(_R_CONV, 2 * GROUP_W)
    y = zc[:, :GROUP_W] * _sigmoid(zc[:, GROUP_W:])
    ybuf_ref[CONV_HALO:CONV_HALO + tm, :] = y
    off = CONV_HALO - (CONV_KERNEL - 1)
    conv = jnp.zeros((tm, GROUP_W), F32) + cb_ref[...]
    for r in range(8):
        z = None
        for q in range((off + CONV_KERNEL - 1 - r) // 8 + 1):
            t = 8 * q + r - off
            if t < 0:
                continue
            term = cw_ref[t:t + 1, :] * ybuf_ref[8 * q:8 * q + tm + 8, :]
            z = term if z is None else z + term
        conv += z[r:r + tm, :]
    ybuf_ref[0:CONV_HALO, :] = ybuf_ref[tm:tm + CONV_HALO, :]
    ob_ref[r0:r0 + tm, :] = _silu(_layer_norm(conv, cng_ref[...], cnb_ref[...])).astype(BF16)

    gz = _gelu_tanh(zr(_R_GMLP, 2 * GROUP_W))
    u = gz[:, :GROUP_W]
    vn = _layer_norm(gz[:, GROUP_W:], gng_ref[...], gnb_ref[...]).astype(BF16)
    tri = (lax.broadcasted_iota(jnp.int32, (N_HEADS * CHUNK, CHUNK), 0) % CHUNK
           >= lax.broadcasted_iota(jnp.int32, (N_HEADS * CHUNK, CHUNK), 1))
    ws = jnp.where(tri, gws_ref[...], 0.0).astype(BF16)
    lane_head = lax.broadcasted_iota(jnp.int32, (CHUNK, GROUP_W), 1) // HEAD_DIM
    for ci in range(tm // CHUNK):
        rows = slice(ci * CHUNK, (ci + 1) * CHUNK)
        res = _dot(ws, vn[rows, :])
        s = res[0:CHUNK, :]
        for hd in range(1, N_HEADS):
            s = jnp.where(lane_head == hd, res[hd * CHUNK:(hd + 1) * CHUNK, :], s)
        oc_ref[r0 + ci * CHUNK:r0 + (ci + 1) * CHUNK, :] = (u[rows, :] * (s + gbs_ref[...])).astype(BF16)

    head_lane = lax.broadcasted_iota(jnp.int32, (tm, 128), 1) < N_HEADS

    def pack3(x):
        hi, mid, lo = _split3(jnp.where(head_lane, x, 0.0))
        return (hi.astype(F32) + pltpu.roll(mid.astype(F32), N_HEADS, axis=1)
                + pltpu.roll(lo.astype(F32), 2 * N_HEADS, axis=1)).astype(BF16)

    lf = _log_sigmoid(zr(_R_F, 128) + fb_ref[...]) * LOG2E
    c3 = _dot(ltri_ref[...], pack3(lf))
    cum = (c3 + pltpu.roll(c3, 128 - N_HEADS, axis=1)
           + pltpu.roll(c3, 128 - 2 * N_HEADS, axis=1)) + carry
    carry_ref[...] = jnp.broadcast_to(cum[tm - 1:tm, :], carry_ref.shape)
    cpk = pack3(cum)
    aug_k = _dot(cpk, ek_ref[...]) + onek_ref[...]
    aug_q = _dot_nt(eq_ref[...], cpk) + oneq_ref[...]
    zk = zr(_R_KD, N_HEADS * FOX_PAD)
    zq = zt(_T_QD, N_HEADS * FOX_PAD)
    for hd in range(N_HEADS):
        hs = slice(hd * FOX_PAD, (hd + 1) * FOX_PAD)
        blk = zk[:, hs]
        ss = jnp.sum(blk * blk, axis=-1, keepdims=True)
        kd_ref[r0:r0 + tm, hs] = (blk * lax.rsqrt(ss * (1.0 / HEAD_DIM) + EPS) * gkd_ref[:, hs]
                                  + aug_k[:, hs]).astype(BF16)
        blk = zq[hs, :]
        ss = jnp.sum(blk * blk, axis=0, keepdims=True)
        qd = (blk * lax.rsqrt(ss * (1.0 / HEAD_DIM) + EPS) * gqd_ref[hs, :] + aug_q[hs, :]).astype(BF16)
        for cb in range(tm // TQ):
            qd_ref[sub * (tm // TQ) + cb, hs, :] = qd[:, cb * TQ:(cb + 1) * TQ]

    zk = zr(_R_KA, GROUP_W)
    ss = _dot((zk * zk).astype(BF16), bd32_ref[...])
    ka_ref[r0:r0 + tm, :] = (zk * lax.rsqrt(ss * (1.0 / DIFF_QK_DIM) + EPS) * gka_ref[...]).astype(BF16)
    zq = zt(_T_QA, GROUP_W)
    ss = _dot(bd32_ref[...], (zq * zq).astype(BF16))
    qa = (zq * lax.rsqrt(ss * (1.0 / DIFF_QK_DIM) + EPS) * gqa_ref[...]).astype(BF16)
    for cb in range(tm // TQ):
        qa_ref[sub * (tm // TQ) + cb] = qa[:, cb * TQ:(cb + 1) * TQ]

    vrow = lax.broadcasted_iota(jnp.int32, (N_HEADS * V_ROWS, tm), 0) % V_ROWS
    ones_rows = (vrow >= HEAD_DIM).astype(F32)
    va = (zt(_T_VA, N_HEADS * V_ROWS) + ones_rows).astype(BF16)
    vd = (zt(_T_VD, N_HEADS * V_ROWS) + ones_rows).astype(BF16)
    for cb in range(tm // TK):
        va_ref[sub * (tm // TK) + cb] = va[:, cb * TK:(cb + 1) * TK]
        vd_ref[sub * (tm // TK) + cb] = vd[:, cb * TK:(cb + 1) * TK]


def _proj_constants(tm):
    def block_diag(n, blk):
        idx = np.arange(n) // blk
        return (idx[:, None] == idx[None, :]).astype(np.float32)

    ltri = np.tril(np.ones((tm, tm), np.float32))
    ek = np.zeros((128, N_HEADS * FOX_PAD), np.float32)
    eq = np.zeros((N_HEADS * FOX_PAD, 128), np.float32)
    onek = np.zeros((1, N_HEADS * FOX_PAD), np.float32)
    oneq = np.zeros((N_HEADS * FOX_PAD, 1), np.float32)
    for hd in range(N_HEADS):
        base = hd * FOX_PAD + HEAD_DIM
        for p in range(3):
            ek[p * N_HEADS + hd, base + p] = -1.0
            eq[base + 3 + p, p * N_HEADS + hd] = 1.0
        onek[0, base + 3:base + 6] = 1.0
        oneq[base:base + 3, 0] = 1.0
    return dict(
        bd32=jnp.asarray(block_diag(GROUP_W, DIFF_QK_DIM), BF16),
        ltri=jnp.asarray(ltri, BF16), ek=jnp.asarray(ek, BF16), eq=jnp.asarray(eq, BF16),
        onek=jnp.asarray(onek), oneq=jnp.asarray(oneq))


def _proj_weights(w_in):
    d = w_in.shape[0]
    c0 = 0
    qa = w_in[:, c0:c0 + GROUP_W]; c0 += GROUP_W
    ka = w_in[:, c0:c0 + GROUP_W]; c0 += GROUP_W
    va = w_in[:, c0:c0 + GROUP_W]; c0 += GROUP_W
    conv = w_in[:, c0:c0 + 2 * GROUP_W]; c0 += 2 * GROUP_W
    gmlp = w_in[:, c0:c0 + 2 * GROUP_W]; c0 += 2 * GROUP_W
    qd = w_in[:, c0:c0 + GROUP_W]; c0 += GROUP_W
    kd = w_in[:, c0:c0 + GROUP_W]; c0 += GROUP_W
    vd = w_in[:, c0:c0 + GROUP_W]; c0 += GROUP_W
    wf = w_in[:, c0:c0 + N_HEADS]

    def pad_heads(w, width):
        w = w.reshape(d, N_HEADS, HEAD_DIM)
        w = jnp.pad(w, ((0, 0), (0, 0), (0, width - HEAD_DIM)))
        return w.reshape(d, N_HEADS * width)

    w_rm = jnp.concatenate(
        [ka, conv, gmlp, pad_heads(kd, FOX_PAD), jnp.pad(wf, ((0, 0), (0, 128 - N_HEADS)))], axis=1)
    w_t = jnp.concatenate(
        [qa, pad_heads(va, V_ROWS), pad_heads(qd, FOX_PAD), pad_heads(vd, V_ROWS)], axis=1).T
    return w_rm.astype(BF16), w_t.astype(BF16)


def _proj_call(h, mod, g, w_in, consts, diff_qk_g, fox_qk_g, fox_fb,
               conv_w, conv_b, conv_ng, conv_nb, gmlp_ng, gmlp_nb, gmlp_ws, gmlp_bs, *, seq):
    n, d = h.shape
    tm = TM_PROJ
    tiles_per_seq = seq // tm
    nqb, nkb = tm // TQ, tm // TK
    w_rm, w_t = _proj_weights(w_in)

    diff_scale = DIFF_QK_DIM ** -0.5 * LOG2E
    fox_scale = HEAD_DIM ** -0.5 * LOG2E
    gka = jnp.tile(diff_qk_g[1], 2 * N_HEADS).reshape(1, GROUP_W)
    gqa = (jnp.tile(diff_qk_g[0], 2 * N_HEADS) * diff_scale).reshape(GROUP_W, 1)
    pad = jnp.zeros((FOX_PAD - HEAD_DIM,), F32)
    gkd = jnp.tile(jnp.concatenate([fox_qk_g[1], pad]), N_HEADS).reshape(1, N_HEADS * FOX_PAD)
    gqd = jnp.tile(jnp.concatenate([fox_qk_g[0] * fox_scale, pad]), N_HEADS).reshape(N_HEADS * FOX_PAD, 1)
    fb = jnp.pad(fox_fb, (0, 128 - N_HEADS)).reshape(1, 128)
    cw = jnp.pad(conv_w, ((0, 32 - CONV_KERNEL), (0, 0)))
    gws = gmlp_ws.reshape(N_HEADS * CHUNK, CHUNK)
    gbs = jnp.repeat(gmlp_bs.T, HEAD_DIM, axis=1)

    full = lambda a: _resident(a.shape)
    row = lambda i: (i, 0)
    small = [consts["bd32"], consts["ltri"], consts["ek"], consts["eq"],
             gka, gqa, gkd, gqd, consts["onek"], consts["oneq"], fb,
             cw, conv_b.reshape(1, -1), conv_ng.reshape(1, -1), conv_nb.reshape(1, -1),
             gmlp_ng.reshape(1, -1), gmlp_nb.reshape(1, -1), gws, gbs]
    in_specs = [
        pl.BlockSpec((tm, d), row),
        pl.BlockSpec((1, N_MOD, d), lambda i: (i // tiles_per_seq, 0, 0)),
        full(g.reshape(1, d)),
        full(w_rm), full(w_t),
    ] + [full(a) for a in small]
    blk3 = lambda i: (i, 0, 0)
    out_specs = [
        pl.BlockSpec((tm, GROUP_W), row),
        pl.BlockSpec((nqb, GROUP_W, TQ), blk3),
        pl.BlockSpec((nkb, N_HEADS * V_ROWS, TK), blk3),
        pl.BlockSpec((tm, N_HEADS * FOX_PAD), row),
        pl.BlockSpec((nqb, N_HEADS * FOX_PAD, TQ), blk3),
        pl.BlockSpec((nkb, N_HEADS * V_ROWS, TK), blk3),
        pl.BlockSpec((tm, GROUP_W), row),
        pl.BlockSpec((tm, GROUP_W), row),
    ]
    out_shape = [
        jax.ShapeDtypeStruct((n, GROUP_W), BF16),
        jax.ShapeDtypeStruct((n // TQ, GROUP_W, TQ), BF16),
        jax.ShapeDtypeStruct((n // TK, N_HEADS * V_ROWS, TK), BF16),
        jax.ShapeDtypeStruct((n, N_HEADS * FOX_PAD), BF16),
        jax.ShapeDtypeStruct((n // TQ, N_HEADS * FOX_PAD, TQ), BF16),
        jax.ShapeDtypeStruct((n // TK, N_HEADS * V_ROWS, TK), BF16),
        jax.ShapeDtypeStruct((n, GROUP_W), BF16),
        jax.ShapeDtypeStruct((n, GROUP_W), BF16),
    ]
    return pl.pallas_call(
        functools.partial(_proj_kernel, tiles_per_seq=tiles_per_seq),
        grid=(n // tm,),
        in_specs=in_specs,
        out_specs=out_specs,
        out_shape=out_shape,
        scratch_shapes=[pltpu.VMEM((CONV_HALO + tm + CONV_TAIL, GROUP_W), F32),
                        pltpu.VMEM((8, 128), F32)],
        compiler_params=pltpu.CompilerParams(
            dimension_semantics=("arbitrary",), vmem_limit_bytes=VMEM_LIMIT),
        name="in_proj",
    )(h, mod, g.reshape(1, d), w_rm, w_t, *small)


def _attend(n_chains, q_of, k_of, v_of, s_refs, m_ref, acc_ref, i):
    m_ref[...] = jnp.full(m_ref.shape, NEG_INF, F32)
    acc_ref[...] = jnp.zeros(acc_ref.shape, F32)
    all_q, upper_q = slice(0, TQ), slice(TK, TQ)

    def scores(j, s_ref, ch, qs):
        s_ref[ch, :, qs] = _dot(k_of(ch, j), q_of(ch)[:, qs])

    def update(j, s_ref, ch, masked, qs):
        width = qs.stop - qs.start
        s = s_ref[ch, :, qs]
        if masked:
            kp = lax.broadcasted_iota(jnp.int32, (TK, width), 0) + (j * TK - i * TQ - qs.start)
            s = jnp.where(kp <= lax.broadcasted_iota(jnp.int32, (TK, width), 1), s, NEG_INF)
        m_full = m_ref[ch]
        m_old = m_full[:, qs]
        m_new = jnp.maximum(m_old, jnp.max(s, axis=0, keepdims=True))
        p = jnp.exp2(s - m_new).astype(BF16)
        acc_ref[ch, :, qs] = acc_ref[ch, :, qs] * jnp.exp2(m_old - m_new) + _dot(v_of(ch, j), p)
        m_ref[ch] = m_new if width == TQ else jnp.concatenate([m_full[:, :qs.start], m_new], axis=1)

    def step(j_next, s_next, j_cur, s_cur, masked, qs_next=all_q, qs_cur=all_q):
        for ch in range(n_chains):
            if j_next is not None:
                scores(j_next, s_next, ch, qs_next)
            update(j_cur, s_cur, ch, masked, qs_cur)

    n_pairs = i
    for ch in range(n_chains):
        scores(0, s_refs[0], ch, all_q)

    def body(jj, carry):
        step(2 * jj + 1, s_refs[1], 2 * jj, s_refs[0], False)
        step(2 * jj + 2, s_refs[0], 2 * jj + 1, s_refs[1], False)
        return carry

    lax.fori_loop(0, n_pairs, body, 0)
    step(2 * n_pairs + 1, s_refs[1], 2 * n_pairs, s_refs[0], True, qs_next=upper_q)
    step(None, None, 2 * n_pairs + 1, s_refs[1], True, qs_cur=upper_q)


def _normalised(acc_ref, ch):
    return acc_ref[ch, 0:HEAD_DIM, :] / acc_ref[ch, HEAD_DIM:HEAD_DIM + 1, :]


def _diff_attn_kernel(q_ref, k_ref, v_ref, lam_ref, og_ref, o_ref,
                      qm_ref, s0_ref, s1_ref, m_ref, acc_ref, ot_ref, *, lam_init):
    i = pl.program_id(1)
    q_t = q_ref[0]
    rows = lax.broadcasted_iota(jnp.int32, (GROUP_W, TQ), 0)
    for ch in range(2 * N_HEADS):
        lo = ch * DIFF_QK_DIM
        qm_ref[ch] = jnp.where((rows >= lo) & (rows < lo + DIFF_QK_DIM), q_t, jnp.zeros_like(q_t))
    _attend(2 * N_HEADS, lambda ch: qm_ref[ch], lambda ch, j: k_ref[j],
            lambda ch, j: v_ref[j, (ch // 2) * V_ROWS:(ch // 2 + 1) * V_ROWS, :],
            (s0_ref, s1_ref), m_ref, acc_ref, i)
    lv = lam_ref[...]
    lam = (jnp.exp(jnp.sum(lv[0:1, :] * lv[1:2, :], axis=-1, keepdims=True))
           - jnp.exp(jnp.sum(lv[2:3, :] * lv[3:4, :], axis=-1, keepdims=True)) + lam_init)
    for hd in range(N_HEADS):
        o = _normalised(acc_ref, 2 * hd) - lam * _normalised(acc_ref, 2 * hd + 1)
        ms = jnp.mean(o * o, axis=0, keepdims=True)
        ot_ref[hd * HEAD_DIM:(hd + 1) * HEAD_DIM, :] = o * lax.rsqrt(ms + EPS) * og_ref[...]
    o_ref[...] = ot_ref[...].T.astype(BF16)


def _fox_attn_kernel(q_ref, k_ref, v_ref, o_ref, s0_ref, s1_ref, m_ref, acc_ref, ot_ref):
    i = pl.program_id(1)
    _attend(N_HEADS, lambda ch: q_ref[0, ch * FOX_PAD:(ch + 1) * FOX_PAD, :],
            lambda ch, j: k_ref[j, :, ch * FOX_PAD:(ch + 1) * FOX_PAD],
            lambda ch, j: v_ref[j, ch * V_ROWS:(ch + 1) * V_ROWS, :],
            (s0_ref, s1_ref), m_ref, acc_ref, i)
    for hd in range(N_HEADS):
        ot_ref[hd * HEAD_DIM:(hd + 1) * HEAD_DIM, :] = _normalised(acc_ref, hd)
    o_ref[...] = ot_ref[...].T.astype(BF16)


def _attn_call(kernel, q_t, k, v_t, extra, *, n_chains, masked_q, batch, seq, name):
    nq, nk = seq // TQ, seq // TK
    feat = q_t.shape[1]
    k = k.reshape(batch * nk, TK, feat)
    scratch = [pltpu.VMEM((n_chains, feat, TQ), BF16)] if masked_q else []
    scratch += [pltpu.VMEM((n_chains, TK, TQ), F32), pltpu.VMEM((n_chains, TK, TQ), F32),
                pltpu.VMEM((n_chains, 1, TQ), F32), pltpu.VMEM((n_chains, V_ROWS, TQ), F32),
                pltpu.VMEM((GROUP_W, TQ), F32)]
    in_specs = [
        pl.BlockSpec((1, feat, TQ), lambda b, i: (b * nq + i, 0, 0)),
        pl.BlockSpec((nk, TK, feat), lambda b, i: (b, 0, 0)),
        pl.BlockSpec((nk, N_HEADS * V_ROWS, TK), lambda b, i: (b, 0, 0)),
    ] + [pl.BlockSpec(a.shape, lambda b, i, _nd=a.ndim: (0,) * _nd) for a in extra]
    return pl.pallas_call(
        kernel,
        grid=(batch, nq),
        in_specs=in_specs,
        out_specs=pl.BlockSpec((TQ, GROUP_W), lambda b, i: (b * nq + i, 0)),
        out_shape=jax.ShapeDtypeStruct((batch * seq, GROUP_W), BF16),
        scratch_shapes=scratch,
        compiler_params=pltpu.CompilerParams(
            dimension_semantics=("arbitrary", "arbitrary"), vmem_limit_bytes=VMEM_LIMIT),
        name=name,
    )(q_t, k, v_t, *extra)


def kernel(x, c, ada_w, ada_b, norm_g, ffn1_w_in, ffn1_w_out, ffn2_w_in, ffn2_w_out, w_in, w_out, diff_qk_g, diff_lambda, diff_out_g, conv_w, conv_b, conv_norm_g, conv_norm_b, gmlp_norm_g, gmlp_norm_b, gmlp_ws, gmlp_bs, fox_qk_g, fox_fb):
    batch, seq, d = x.shape
    depth = ada_w.shape[0]
    assert seq % TM_PROJ == 0 and seq % TM_FFN == 0 and TM_PROJ % TQ == 0
    assert w_in.shape[-1] == 10 * GROUP_W + N_HEADS and w_out.shape[1] == 4 * GROUP_W
    assert ffn1_w_out.shape[1] % TF_FFN == 0

    mods = _ada_call(c, ada_w, ada_b)
    consts = _proj_constants(TM_PROJ)
    h = x.reshape(batch * seq, d)
    for l in range(depth):
        mod = mods[l]
        h = _ffn_call(h, mod, norm_g[l, 0], ffn1_w_in[l].astype(BF16), ffn1_w_out[l].astype(BF16),
                      seq=seq, mod_row=0)
        ka, qa_t, va_t, kd, qd_t, vd_t, o_b, o_c = _proj_call(
            h, mod, norm_g[l, 1], w_in[l], consts, diff_qk_g[l], fox_qk_g[l], fox_fb[l],
            conv_w[l], conv_b[l], conv_norm_g[l], conv_norm_b[l],
            gmlp_norm_g[l], gmlp_norm_b[l], gmlp_ws[l], gmlp_bs[l], seq=seq)
        lam_init = 0.8 - 0.6 * math.exp(-0.3 * l)
        og = (diff_out_g[l] * (1.0 - lam_init)).reshape(HEAD_DIM, 1)
        o_a = _attn_call(functools.partial(_diff_attn_kernel, lam_init=lam_init),
                         qa_t, ka, va_t, [diff_lambda[l], og], n_chains=2 * N_HEADS, masked_q=True,
                         batch=batch, seq=seq, name="diff_attn")
        o_d = _attn_call(_fox_attn_kernel, qd_t, kd, vd_t, [], n_chains=N_HEADS, masked_q=False,
                         batch=batch, seq=seq, name="fox_attn")
        h = _ffn_call(h, mod, norm_g[l, 2], ffn2_w_in[l].astype(BF16), ffn2_w_out[l].astype(BF16),
                      seq=seq, mod_row=6, mix=(o_a, o_b, o_c, o_d, w_out[l].astype(BF16)))
    return h.reshape(batch, seq, d)
```

```python
import functools
import math

import numpy as np
import jax
import jax.numpy as jnp
from jax import lax
from jax.experimental import pallas as pl
from jax.experimental.pallas import tpu as pltpu

F32 = jnp.float32
BF16 = jnp.bfloat16

HEAD_DIM = 64
N_HEADS = 4
GROUP_W = N_HEADS * HEAD_DIM
DIFF_QK_DIM = HEAD_DIM // 2
CONV_KERNEL = 31
CHUNK = 128
N_MOD = 9
EPS = 1e-6
NEG_INF = -1e30
LOG2E = math.log2(math.e)

V_ROWS = HEAD_DIM + 16
FOX_PAD = 128
CONV_HALO = 32
CONV_TAIL = 16

TQ = 512
TK = TQ // 2
TM_PROJ = 512
PROJ_SUB = 2
TM_FFN = 512
TF_FFN = 256
VMEM_LIMIT = 56 * 1024 * 1024


def _dot(a, b):
    return jnp.dot(a, b, preferred_element_type=F32)


def _dot_nt(a, b):
    return lax.dot_general(a, b, (((1,), (1,)), ((), ())), preferred_element_type=F32)


def _sigmoid(x):
    return 1.0 / (1.0 + jnp.exp(-x))


def _silu(x):
    return x * _sigmoid(x)


def _gelu_tanh(x):
    return 0.5 * x * (1.0 + jnp.tanh(math.sqrt(2.0 / math.pi) * (x + 0.044715 * (x * x * x))))


def _log_sigmoid(x):
    return jnp.minimum(x, 0.0) - jnp.log1p(jnp.exp(-jnp.abs(x)))


def _norm_mod(x, g, shift, scale):
    ms = jnp.mean(x * x, axis=-1, keepdims=True)
    y = x * lax.rsqrt(ms + EPS) * g
    return y * (1.0 + scale) + shift


def _layer_norm(x, g, b):
    mu = jnp.mean(x, axis=-1, keepdims=True)
    xc = x - mu
    var = jnp.mean(xc * xc, axis=-1, keepdims=True)
    return xc * lax.rsqrt(var + EPS) * g + b


def _split3(x):
    hi = x.astype(BF16)
    r = x - hi.astype(F32)
    mid = r.astype(BF16)
    lo = (r - mid.astype(F32)).astype(BF16)
    return hi, mid, lo


def _ada_kernel(c_ref, w_ref, b_ref, o_ref):
    c = c_ref[...]
    cond = _silu(c).astype(BF16)
    o_ref[0] = _dot(cond, w_ref[0].astype(BF16)) + b_ref[0]


def _ada_call(c, ada_w, ada_b):
    n_layers, d, nm = ada_w.shape
    b = c.shape[0]
    bp = 16
    tn = nm // 8
    c_pad = jnp.zeros((bp, d), F32).at[:b].set(c)
    out = pl.pallas_call(
        _ada_kernel,
        grid=(n_layers, nm // tn),
        in_specs=[
            pl.BlockSpec((bp, d), lambda l, j: (0, 0)),
            pl.BlockSpec((1, d, tn), lambda l, j: (l, 0, j)),
            pl.BlockSpec((1, 1, tn), lambda l, j: (l, 0, j)),
        ],
        out_specs=pl.BlockSpec((1, bp, tn), lambda l, j: (l, 0, j)),
        out_shape=jax.ShapeDtypeStruct((n_layers, bp, nm), F32),
        compiler_params=pltpu.CompilerParams(
            dimension_semantics=("arbitrary", "arbitrary"), vmem_limit_bytes=VMEM_LIMIT),
        name="ada_mod",
    )(c_pad, ada_w, ada_b.reshape(n_layers, 1, nm))
    return out[:, :b].reshape(n_layers, b, N_MOD, d)


def _ffn_kernel(*refs, with_mix, mod_row, d_ff):
    if with_mix:
        (h_ref, mod_ref, g_ref, oa_ref, ob_ref, oc_ref, od_ref, wmix_ref,
         win_ref, wout_ref, out_ref, act_ref) = refs
    else:
        (h_ref, mod_ref, g_ref, win_ref, wout_ref, out_ref, act_ref) = refs
    x = h_ref[...]
    if with_mix:
        mix = _dot(oa_ref[...], wmix_ref[0 * GROUP_W:1 * GROUP_W, :])
        mix += _dot(ob_ref[...], wmix_ref[1 * GROUP_W:2 * GROUP_W, :])
        mix += _dot(oc_ref[...], wmix_ref[2 * GROUP_W:3 * GROUP_W, :])
        mix += _dot(od_ref[...], wmix_ref[3 * GROUP_W:4 * GROUP_W, :])
        x = x + mod_ref[0, mod_row - 1:mod_row, :] * mix
    hn = _norm_mod(x, g_ref[...], mod_ref[0, mod_row:mod_row + 1, :],
                   mod_ref[0, mod_row + 1:mod_row + 2, :]).astype(BF16)
    for c in range(d_ff // TF_FFN):
        cols = slice(c * TF_FFN, (c + 1) * TF_FFN)
        gate = _dot(hn, win_ref[:, cols])
        up = _dot(hn, win_ref[:, d_ff + c * TF_FFN:d_ff + (c + 1) * TF_FFN])
        act_ref[:, cols] = (_silu(gate) * up).astype(BF16)
    out_ref[...] = x + (0.5 * mod_ref[0, mod_row + 2:mod_row + 3, :]) * _dot(act_ref[...], wout_ref[...])


def _resident(shape):
    return pl.BlockSpec(shape, lambda i, _nd=len(shape): (0,) * _nd, pipeline_mode=pl.Buffered(1))


def _ffn_call(h, mod, g, w_in, w_out, *, seq, mod_row, mix=None):
    n, d = h.shape
    d_ff = w_out.shape[0]
    tm = TM_FFN
    tiles_per_seq = seq // tm
    with_mix = mix is not None

    row = lambda i: (i, 0)
    in_specs = [
        pl.BlockSpec((tm, d), row),
        pl.BlockSpec((1, N_MOD, d), lambda i: (i // tiles_per_seq, 0, 0)),
        _resident((1, d)),
    ]
    args = [h, mod, g.reshape(1, d)]
    if with_mix:
        o_a, o_b, o_c, o_d, w_mix = mix
        in_specs += [pl.BlockSpec((tm, GROUP_W), row)] * 4 + [_resident((d, d))]
        args += [o_a, o_b, o_c, o_d, w_mix]
    in_specs += [_resident(w_in.shape), _resident(w_out.shape)]
    args += [w_in, w_out]
    return pl.pallas_call(
        functools.partial(_ffn_kernel, with_mix=with_mix, mod_row=mod_row, d_ff=d_ff),
        grid=(n // tm,),
        in_specs=in_specs,
        out_specs=pl.BlockSpec((tm, d), row),
        out_shape=jax.ShapeDtypeStruct((n, d), F32),
        scratch_shapes=[pltpu.VMEM((tm, d_ff), BF16)],
        compiler_params=pltpu.CompilerParams(
            dimension_semantics=("arbitrary",), vmem_limit_bytes=VMEM_LIMIT),
        name="ffn_mix" if with_mix else "ffn",
    )(*args)


_R_KA = 0
_R_CONV = _R_KA + GROUP_W
_R_GMLP = _R_CONV + 2 * GROUP_W
_R_KD = _R_GMLP + 2 * GROUP_W
_R_F = _R_KD + N_HEADS * FOX_PAD
_R_COLS = _R_F + 128
_T_QA = 0
_T_VA = _T_QA + GROUP_W
_T_QD = _T_VA + N_HEADS * V_ROWS
_T_VD = _T_QD + N_HEADS * FOX_PAD
_T_ROWS = _T_VD + N_HEADS * V_ROWS


def _proj_kernel(h_ref, mod_ref, g_ref, wrm_ref, wt_ref,
                 bd32_ref, ltri_ref, ek_ref, eq_ref,
                 gka_ref, gqa_ref, gkd_ref, gqd_ref, onek_ref, oneq_ref, fb_ref,
                 cw_ref, cb_ref, cng_ref, cnb_ref,
                 gng_ref, gnb_ref, gws_ref, gbs_ref,
                 ka_ref, qa_ref, va_ref, kd_ref, qd_ref, vd_ref, ob_ref, oc_ref,
                 ybuf_ref, carry_ref, *, steps_per_seq):
    tm = h_ref.shape[0] // PROJ_SUB
    i = pl.program_id(0)
    consts = (bd32_ref, ltri_ref, ek_ref, eq_ref,
              gka_ref, gqa_ref, gkd_ref, gqd_ref, onek_ref, oneq_ref, fb_ref,
              cw_ref, cb_ref, cng_ref, cnb_ref, gng_ref, gnb_ref, gws_ref, gbs_ref)
    outs = (ka_ref, qa_ref, va_ref, kd_ref, qd_ref, vd_ref, ob_ref, oc_ref)

    @pl.when(i % steps_per_seq == 0)
    def _():
        ybuf_ref[0:CONV_HALO, :] = jnp.zeros((CONV_HALO, GROUP_W), F32)
        ybuf_ref[CONV_HALO + tm:CONV_HALO + tm + CONV_TAIL, :] = jnp.zeros((CONV_TAIL, GROUP_W), F32)
        carry_ref[...] = jnp.zeros_like(carry_ref)

    projected = []
    for sub in range(PROJ_SUB):
        hn = _norm_mod(h_ref[sub * tm:(sub + 1) * tm, :], g_ref[...],
                       mod_ref[0, 3:4, :], mod_ref[0, 4:5, :]).astype(BF16)
        projected.append((_dot(hn, wrm_ref[...]),
                          _dot_nt(wt_ref[...], hn)))
    for sub, (zr_all, zt_all) in enumerate(projected):
        _proj_post(sub, zr_all, zt_all, consts, outs, ybuf_ref, carry_ref)


def _proj_post(sub, zr_all, zt_all, consts, outs, ybuf_ref, carry_ref):
    (bd32_ref, ltri_ref, ek_ref, eq_ref,
     gka_ref, gqa_ref, gkd_ref, gqd_ref, onek_ref, oneq_ref, fb_ref,
     cw_ref, cb_ref, cng_ref, cnb_ref, gng_ref, gnb_ref, gws_ref, gbs_ref) = consts
    ka_ref, qa_ref, va_ref, kd_ref, qd_ref, vd_ref, ob_ref, oc_ref = outs
    tm = zr_all.shape[0]
    r0 = sub * tm
    zr = lambda col, width: zr_all[:, col:col + width]
    zt = lambda row, height: zt_all[row:row + height, :]
    carry = carry_ref[0:1, :]

    zc = zr(_R_CONV, 2 * GROUP_W)
    y = zc[:, :GROUP_W] * _sigmoid(zc[:, GROUP_W:])
    ybuf_ref[CONV_HALO:CONV_HALO + tm, :] = y
    off = CONV_HALO - (CONV_KERNEL - 1)
    conv = jnp.zeros((tm, GROUP_W), F32) + cb_ref[...]
    for r in range(8):
        z = None
        for q in range((off + CONV_KERNEL - 1 - r) // 8 + 1):
            t = 8 * q + r - off
            if t < 0:
                continue
            term = cw_ref[t:t + 1, :] * ybuf_ref[8 * q:8 * q + tm + 8, :]
            z = term if z is None else z + term
        conv += z[r:r + tm, :]
    ybuf_ref[0:CONV_HALO, :] = ybuf_ref[tm:tm + CONV_HALO, :]
    ob_ref[r0:r0 + tm, :] = _silu(_layer_norm(conv, cng_ref[...], cnb_ref[...])).astype(BF16)

    gz = _gelu_tanh(zr(_R_GMLP, 2 * GROUP_W))
    u = gz[:, :GROUP_W]
    vn = _layer_norm(gz[:, GROUP_W:], gng_ref[...], gnb_ref[...]).astype(BF16)
    tri = (lax.broadcasted_iota(jnp.int32, (N_HEADS * CHUNK, CHUNK), 0) % CHUNK
           >= lax.broadcasted_iota(jnp.int32, (N_HEADS * CHUNK, CHUNK), 1))
    ws = jnp.where(tri, gws_ref[...], 0.0).astype(BF16)
    lane_head = lax.broadcasted_iota(jnp.int32, (CHUNK, GROUP_W), 1) // HEAD_DIM
    for ci in range(tm // CHUNK):
        rows = slice(ci * CHUNK, (ci + 1) * CHUNK)
        res = _dot(ws, vn[rows, :])
        s = res[0:CHUNK, :]
        for hd in range(1, N_HEADS):
            s = jnp.where(lane_head == hd, res[hd * CHUNK:(hd + 1) * CHUNK, :], s)
        oc_ref[r0 + ci * CHUNK:r0 + (ci + 1) * CHUNK, :] = (u[rows, :] * (s + gbs_ref[...])).astype(BF16)

    head_lane = lax.broadcasted_iota(jnp.int32, (tm, 128), 1) < N_HEADS

    def pack3(x):
        hi, mid, lo = _split3(jnp.where(head_lane, x, 0.0))
        return (hi.astype(F32) + pltpu.roll(mid.astype(F32), N_HEADS, axis=1)
                + pltpu.roll(lo.astype(F32), 2 * N_HEADS, axis=1)).astype(BF16)

    lf = _log_sigmoid(zr(_R_F, 128) + fb_ref[...]) * LOG2E
    c3 = _dot(ltri_ref[...], pack3(lf))
    cum = (c3 + pltpu.roll(c3, 128 - N_HEADS, axis=1)
           + pltpu.roll(c3, 128 - 2 * N_HEADS, axis=1)) + carry
    carry_ref[...] = jnp.broadcast_to(cum[tm - 1:tm, :], carry_ref.shape)
    cpk = pack3(cum)
    aug_k = _dot(cpk, ek_ref[...]) + onek_ref[...]
    aug_q = _dot_nt(eq_ref[...], cpk) + oneq_ref[...]
    zk = zr(_R_KD, N_HEADS * FOX_PAD)
    zq = zt(_T_QD, N_HEADS * FOX_PAD)
    for hd in range(N_HEADS):
        hs = slice(hd * FOX_PAD, (hd + 1) * FOX_PAD)
        blk = zk[:, hs]
        ss = jnp.sum(blk * blk, axis=-1, keepdims=True)
        kd_ref[r0:r0 + tm, hs] = (blk * lax.rsqrt(ss * (1.0 / HEAD_DIM) + EPS) * gkd_ref[:, hs]
                                  + aug_k[:, hs]).astype(BF16)
        blk = zq[hs, :]
        ss = jnp.sum(blk * blk, axis=0, keepdims=True)
        qd = (blk * lax.rsqrt(ss * (1.0 / HEAD_DIM) + EPS) * gqd_ref[hs, :] + aug_q[hs, :]).astype(BF16)
        for cb in range(tm // TQ):
            qd_ref[sub * (tm // TQ) + cb, hs, :] = qd[:, cb * TQ:(cb + 1) * TQ]

    zk = zr(_R_KA, GROUP_W)
    ss = _dot((zk * zk).astype(BF16), bd32_ref[...])
    ka_ref[r0:r0 + tm, :] = (zk * lax.rsqrt(ss * (1.0 / DIFF_QK_DIM) + EPS) * gka_ref[...]).astype(BF16)
    zq = zt(_T_QA, GROUP_W)
    ss = _dot(bd32_ref[...], (zq * zq).astype(BF16))
    qa = (zq * lax.rsqrt(ss * (1.0 / DIFF_QK_DIM) + EPS) * gqa_ref[...]).astype(BF16)
    for cb in range(tm // TQ):
        qa_ref[sub * (tm // TQ) + cb] = qa[:, cb * TQ:(cb + 1) * TQ]

    vrow = lax.broadcasted_iota(jnp.int32, (N_HEADS * V_ROWS, tm), 0) % V_ROWS
    ones_rows = (vrow >= HEAD_DIM).astype(F32)
    va = (zt(_T_VA, N_HEADS * V_ROWS) + ones_rows).astype(BF16)
    vd = (zt(_T_VD, N_HEADS * V_ROWS) + ones_rows).astype(BF16)
    for cb in range(tm // TK):
        va_ref[sub * (tm // TK) + cb] = va[:, cb * TK:(cb + 1) * TK]
        vd_ref[sub * (tm // TK) + cb] = vd[:, cb * TK:(cb + 1) * TK]


def _proj_constants(tm):
    def block_diag(n, blk):
        idx = np.arange(n) // blk
        return (idx[:, None] == idx[None, :]).astype(np.float32)

    ltri = np.tril(np.ones((tm, tm), np.float32))
    ek = np.zeros((128, N_HEADS * FOX_PAD), np.float32)
    eq = np.zeros((N_HEADS * FOX_PAD, 128), np.float32)
    onek = np.zeros((1, N_HEADS * FOX_PAD), np.float32)
    oneq = np.zeros((N_HEADS * FOX_PAD, 1), np.float32)
    for hd in range(N_HEADS):
        base = hd * FOX_PAD + HEAD_DIM
        for p in range(3):
            ek[p * N_HEADS + hd, base + p] = -1.0
            eq[base + 3 + p, p * N_HEADS + hd] = 1.0
        onek[0, base + 3:base + 6] = 1.0
        oneq[base:base + 3, 0] = 1.0
    return dict(
        bd32=jnp.asarray(block_diag(GROUP_W, DIFF_QK_DIM), BF16),
        ltri=jnp.asarray(ltri, BF16), ek=jnp.asarray(ek, BF16), eq=jnp.asarray(eq, BF16),
        onek=jnp.asarray(onek), oneq=jnp.asarray(oneq))


def _proj_weights(w_in):
    d = w_in.shape[0]
    c0 = 0
    qa = w_in[:, c0:c0 + GROUP_W]; c0 += GROUP_W
    ka = w_in[:, c0:c0 + GROUP_W]; c0 += GROUP_W
    va = w_in[:, c0:c0 + GROUP_W]; c0 += GROUP_W
    conv = w_in[:, c0:c0 + 2 * GROUP_W]; c0 += 2 * GROUP_W
    gmlp = w_in[:, c0:c0 + 2 * GROUP_W]; c0 += 2 * GROUP_W
    qd = w_in[:, c0:c0 + GROUP_W]; c0 += GROUP_W
    kd = w_in[:, c0:c0 + GROUP_W]; c0 += GROUP_W
    vd = w_in[:, c0:c0 + GROUP_W]; c0 += GROUP_W
    wf = w_in[:, c0:c0 + N_HEADS]

    def pad_heads(w, width):
        w = w.reshape(d, N_HEADS, HEAD_DIM)
        w = jnp.pad(w, ((0, 0), (0, 0), (0, width - HEAD_DIM)))
        return w.reshape(d, N_HEADS * width)

    w_rm = jnp.concatenate(
        [ka, conv, gmlp, pad_heads(kd, FOX_PAD), jnp.pad(wf, ((0, 0), (0, 128 - N_HEADS)))], axis=1)
    w_t = jnp.concatenate(
        [qa, pad_heads(va, V_ROWS), pad_heads(qd, FOX_PAD), pad_heads(vd, V_ROWS)], axis=1).T
    return w_rm.astype(BF16), w_t.astype(BF16)


def _proj_call(h, mod, g, w_in, consts, diff_qk_g, fox_qk_g, fox_fb,
               conv_w, conv_b, conv_ng, conv_nb, gmlp_ng, gmlp_nb, gmlp_ws, gmlp_bs, *, seq):
    n, d = h.shape
    sub_tm = TM_PROJ
    tm = PROJ_SUB * sub_tm
    tiles_per_seq = seq // tm
    nqb, nkb = tm // TQ, tm // TK
    w_rm, w_t = _proj_weights(w_in)

    diff_scale = DIFF_QK_DIM ** -0.5 * LOG2E
    fox_scale = HEAD_DIM ** -0.5 * LOG2E
    gka = jnp.tile(diff_qk_g[1], 2 * N_HEADS).reshape(1, GROUP_W)
    gqa = (jnp.tile(diff_qk_g[0], 2 * N_HEADS) * diff_scale).reshape(GROUP_W, 1)
    pad = jnp.zeros((FOX_PAD - HEAD_DIM,), F32)
    gkd = jnp.tile(jnp.concatenate([fox_qk_g[1], pad]), N_HEADS).reshape(1, N_HEADS * FOX_PAD)
    gqd = jnp.tile(jnp.concatenate([fox_qk_g[0] * fox_scale, pad]), N_HEADS).reshape(N_HEADS * FOX_PAD, 1)
    fb = jnp.pad(fox_fb, (0, 128 - N_HEADS)).reshape(1, 128)
    cw = jnp.pad(conv_w, ((0, 32 - CONV_KERNEL), (0, 0)))
    gws = gmlp_ws.reshape(N_HEADS * CHUNK, CHUNK)
    gbs = jnp.repeat(gmlp_bs.T, HEAD_DIM, axis=1)

    full = lambda a: _resident(a.shape)
    row = lambda i: (i, 0)
    small = [consts["bd32"], consts["ltri"], consts["ek"], consts["eq"],
             gka, gqa, gkd, gqd, consts["onek"], consts["oneq"], fb,
             cw, conv_b.reshape(1, -1), conv_ng.reshape(1, -1), conv_nb.reshape(1, -1),
             gmlp_ng.reshape(1, -1), gmlp_nb.reshape(1, -1), gws, gbs]
    in_specs = [
        pl.BlockSpec((tm, d), row),
        pl.BlockSpec((1, N_MOD, d), lambda i: (i // tiles_per_seq, 0, 0)),
        full(g.reshape(1, d)),
        full(w_rm), full(w_t),
    ] + [full(a) for a in small]
    blk3 = lambda i: (i, 0, 0)
    out_specs = [
        pl.BlockSpec((tm, GROUP_W), row),
        pl.BlockSpec((nqb, GROUP_W, TQ), blk3),
        pl.BlockSpec((nkb, N_HEADS * V_ROWS, TK), blk3),
        pl.BlockSpec((tm, N_HEADS * FOX_PAD), row),
        pl.BlockSpec((nqb, N_HEADS * FOX_PAD, TQ), blk3),
        pl.BlockSpec((nkb, N_HEADS * V_ROWS, TK), blk3),
        pl.BlockSpec((tm, GROUP_W), row),
        pl.BlockSpec((tm, GROUP_W), row),
    ]
    out_shape = [
        jax.ShapeDtypeStruct((n, GROUP_W), BF16),
        jax.ShapeDtypeStruct((n // TQ, GROUP_W, TQ), BF16),
        jax.ShapeDtypeStruct((n // TK, N_HEADS * V_ROWS, TK), BF16),
        jax.ShapeDtypeStruct((n, N_HEADS * FOX_PAD), BF16),
        jax.ShapeDtypeStruct((n // TQ, N_HEADS * FOX_PAD, TQ), BF16),
        jax.ShapeDtypeStruct((n // TK, N_HEADS * V_ROWS, TK), BF16),
        jax.ShapeDtypeStruct((n, GROUP_W), BF16),
        jax.ShapeDtypeStruct((n, GROUP_W), BF16),
    ]
    return pl.pallas_call(
        functools.partial(_proj_kernel, steps_per_seq=tiles_per_seq),
        grid=(n // tm,),
        in_specs=in_specs,
        out_specs=out_specs,
        out_shape=out_shape,
        scratch_shapes=[pltpu.VMEM((CONV_HALO + sub_tm + CONV_TAIL, GROUP_W), F32),
                        pltpu.VMEM((8, 128), F32)],
        compiler_params=pltpu.CompilerParams(
            dimension_semantics=("arbitrary",), vmem_limit_bytes=VMEM_LIMIT),
        name="in_proj",
    )(h, mod, g.reshape(1, d), w_rm, w_t, *small)


def _attend(n_chains, q_of, k_of, v_of, s_refs, m_ref, acc_ref, i):
    m_ref[...] = jnp.full(m_ref.shape, NEG_INF, F32)
    acc_ref[...] = jnp.zeros(acc_ref.shape, F32)
    all_q, upper_q = slice(0, TQ), slice(TK, TQ)

    def scores(j, s_ref, ch, qs):
        s_ref[ch, :, qs] = _dot(k_of(ch, j), q_of(ch)[:, qs])

    def update(j, s_ref, ch, masked, qs):
        width = qs.stop - qs.start
        s = s_ref[ch, :, qs]
        if masked:
            kp = lax.broadcasted_iota(jnp.int32, (TK, width), 0) + (j * TK - i * TQ - qs.start)
            s = jnp.where(kp <= lax.broadcasted_iota(jnp.int32, (TK, width), 1), s, NEG_INF)
        m_full = m_ref[ch]
        m_old = m_full[:, qs]
        m_new = jnp.maximum(m_old, jnp.max(s, axis=0, keepdims=True))
        p = jnp.exp2(s - m_new).astype(BF16)
        acc_ref[ch, :, qs] = acc_ref[ch, :, qs] * jnp.exp2(m_old - m_new) + _dot(v_of(ch, j), p)
        m_ref[ch] = m_new if width == TQ else jnp.concatenate([m_full[:, :qs.start], m_new], axis=1)

    def step(j_next, s_next, j_cur, s_cur, masked, qs_next=all_q, qs_cur=all_q):
        for ch in range(n_chains):
            if j_next is not None:
                scores(j_next, s_next, ch, qs_next)
            update(j_cur, s_cur, ch, masked, qs_cur)

    n_pairs = i
    for ch in range(n_chains):
        scores(0, s_refs[0], ch, all_q)

    def body(jj, carry):
        step(2 * jj + 1, s_refs[1], 2 * jj, s_refs[0], False)
        step(2 * jj + 2, s_refs[0], 2 * jj + 1, s_refs[1], False)
        return carry

    lax.fori_loop(0, n_pairs, body, 0)
    step(2 * n_pairs + 1, s_refs[1], 2 * n_pairs, s_refs[0], True, qs_next=upper_q)
    step(None, None, 2 * n_pairs + 1, s_refs[1], True, qs_cur=upper_q)


def _normalised(acc_ref, ch):
    return acc_ref[ch, 0:HEAD_DIM, :] * (1.0 / acc_ref[ch, HEAD_DIM:HEAD_DIM + 1, :])


def _diff_attn_kernel(q_ref, k_ref, v_ref, lam_ref, og_ref, o_ref,
                      qm_ref, s0_ref, s1_ref, m_ref, acc_ref, ot_ref, *, lam_init):
    i = pl.program_id(1)
    q_t = q_ref[0]
    rows = lax.broadcasted_iota(jnp.int32, (GROUP_W, TQ), 0)
    for ch in range(2 * N_HEADS):
        lo = ch * DIFF_QK_DIM
        qm_ref[ch] = jnp.where((rows >= lo) & (rows < lo + DIFF_QK_DIM), q_t, jnp.zeros_like(q_t))
    _attend(2 * N_HEADS, lambda ch: qm_ref[ch], lambda ch, j: k_ref[j],
            lambda ch, j: v_ref[j, (ch // 2) * V_ROWS:(ch // 2 + 1) * V_ROWS, :],
            (s0_ref, s1_ref), m_ref, acc_ref, i)
    lv = lam_ref[...]
    lam = (jnp.exp(jnp.sum(lv[0:1, :] * lv[1:2, :], axis=-1, keepdims=True))
           - jnp.exp(jnp.sum(lv[2:3, :] * lv[3:4, :], axis=-1, keepdims=True)) + lam_init)
    for hd in range(N_HEADS):
        o = _normalised(acc_ref, 2 * hd) - lam * _normalised(acc_ref, 2 * hd + 1)
        ms = jnp.mean(o * o, axis=0, keepdims=True)
        ot_ref[hd * HEAD_DIM:(hd + 1) * HEAD_DIM, :] = o * lax.rsqrt(ms + EPS) * og_ref[...]
    o_ref[...] = ot_ref[...].T.astype(BF16)


def _fox_attn_kernel(q_ref, k_ref, v_ref, o_ref, s0_ref, s1_ref, m_ref, acc_ref, ot_ref):
    i = pl.program_id(1)
    _attend(N_HEADS, lambda ch: q_ref[0, ch * FOX_PAD:(ch + 1) * FOX_PAD, :],
            lambda ch, j: k_ref[j, :, ch * FOX_PAD:(ch + 1) * FOX_PAD],
            lambda ch, j: v_ref[j, ch * V_ROWS:(ch + 1) * V_ROWS, :],
            (s0_ref, s1_ref), m_ref, acc_ref, i)
    for hd in range(N_HEADS):
        ot_ref[hd * HEAD_DIM:(hd + 1) * HEAD_DIM, :] = _normalised(acc_ref, hd)
    o_ref[...] = ot_ref[...].T.astype(BF16)


def _attn_call(kernel, q_t, k, v_t, extra, *, n_chains, masked_q, batch, seq, name):
    nq, nk = seq // TQ, seq // TK
    feat = q_t.shape[1]
    k = k.reshape(batch * nk, TK, feat)
    scratch = [pltpu.VMEM((n_chains, feat, TQ), BF16)] if masked_q else []
    scratch += [pltpu.VMEM((n_chains, TK, TQ), F32), pltpu.VMEM((n_chains, TK, TQ), F32),
                pltpu.VMEM((n_chains, 1, TQ), F32), pltpu.VMEM((n_chains, V_ROWS, TQ), F32),
                pltpu.VMEM((GROUP_W, TQ), F32)]
    in_specs = [
        pl.BlockSpec((1, feat, TQ), lambda b, i: (b * nq + i, 0, 0)),
        pl.BlockSpec((nk, TK, feat), lambda b, i: (b, 0, 0)),
        pl.BlockSpec((nk, N_HEADS * V_ROWS, TK), lambda b, i: (b, 0, 0)),
    ] + [pl.BlockSpec(a.shape, lambda b, i, _nd=a.ndim: (0,) * _nd) for a in extra]
    return pl.pallas_call(
        kernel,
        grid=(batch, nq),
        in_specs=in_specs,
        out_specs=pl.BlockSpec((TQ, GROUP_W), lambda b, i: (b * nq + i, 0)),
        out_shape=jax.ShapeDtypeStruct((batch * seq, GROUP_W), BF16),
        scratch_shapes=scratch,
        compiler_params=pltpu.CompilerParams(
            dimension_semantics=("arbitrary", "arbitrary"), vmem_limit_bytes=VMEM_LIMIT),
        name=name,
    )(q_t, k, v_t, *extra)


def kernel(x, c, ada_w, ada_b, norm_g, ffn1_w_in, ffn1_w_out, ffn2_w_in, ffn2_w_out, w_in, w_out, diff_qk_g, diff_lambda, diff_out_g, conv_w, conv_b, conv_norm_g, conv_norm_b, gmlp_norm_g, gmlp_norm_b, gmlp_ws, gmlp_bs, fox_qk_g, fox_fb):
    batch, seq, d = x.shape
    depth = ada_w.shape[0]
    assert seq % (PROJ_SUB * TM_PROJ) == 0 and seq % TM_FFN == 0 and TM_PROJ % TQ == 0
    assert w_in.shape[-1] == 10 * GROUP_W + N_HEADS and w_out.shape[1] == 4 * GROUP_W
    assert ffn1_w_out.shape[1] % TF_FFN == 0

    mods = _ada_call(c, ada_w, ada_b)
    consts = _proj_constants(TM_PROJ)
    h = x.reshape(batch * seq, d)
    for l in range(depth):
        mod = mods[l]
        h = _ffn_call(h, mod, norm_g[l, 0], ffn1_w_in[l].astype(BF16), ffn1_w_out[l].astype(BF16),
                      seq=seq, mod_row=0)
        ka, qa_t, va_t, kd, qd_t, vd_t, o_b, o_c = _proj_call(
            h, mod, norm_g[l, 1], w_in[l], consts, diff_qk_g[l], fox_qk_g[l], fox_fb[l],
            conv_w[l], conv_b[l], conv_norm_g[l], conv_norm_b[l],
            gmlp_norm_g[l], gmlp_norm_b[l], gmlp_ws[l], gmlp_bs[l], seq=seq)
        lam_init = 0.8 - 0.6 * math.exp(-0.3 * l)
        og = (diff_out_g[l] * (1.0 - lam_init)).reshape(HEAD_DIM, 1)
        o_a = _attn_call(functools.partial(_diff_attn_kernel, lam_init=lam_init),
                         qa_t, ka, va_t, [diff_lambda[l], og], n_chains=2 * N_HEADS, masked_q=True,
                         batch=batch, seq=seq, name="diff_attn")
        o_d = _attn_call(_fox_attn_kernel, qd_t, kd, vd_t, [], n_chains=N_HEADS, masked_q=False,
                         batch=batch, seq=seq, name="fox_attn")
        h = _ffn_call(h, mod, norm_g[l, 2], ffn2_w_in[l].astype(BF16), ffn2_w_out[l].astype(BF16),
                      seq=seq, mod_row=6, mix=(o_a, o_b, o_c, o_d, w_out[l].astype(BF16)))
    return h.reshape(batch, seq, d)
```

```python
import functools
import math

import numpy as np
import jax
import jax.numpy as jnp
from jax import lax
from jax.experimental import pallas as pl
from jax.experimental.pallas import tpu as pltpu

F32 = jnp.float32
BF16 = jnp.bfloat16

HEAD_DIM = 64
N_HEADS = 4
GROUP_W = N_HEADS * HEAD_DIM
DIFF_QK_DIM = HEAD_DIM // 2
CONV_KERNEL = 31
CHUNK = 128
N_MOD = 9
EPS = 1e-6
NEG_INF = -1e30
LOG2E = math.log2(math.e)

V_ROWS = HEAD_DIM + 16
FOX_PAD = 128
CONV_HALO = 32
CONV_TAIL = 16

TQ = 512
TK = TQ // 2
TM_PROJ = 512
PROJ_SUB = 2
TM_FFN = 512
TF_FFN = 256
VMEM_LIMIT = 56 * 1024 * 1024


def _dot(a, b):
    return jnp.dot(a, b, preferred_element_type=F32)


def _dot_nt(a, b):
    return lax.dot_general(a, b, (((1,), (1,)), ((), ())), preferred_element_type=F32)


def _sigmoid(x):
    return 1.0 / (1.0 + jnp.exp(-x))


def _silu(x):
    return x * _sigmoid(x)


def _gelu_tanh(x):
    return 0.5 * x * (1.0 + jnp.tanh(math.sqrt(2.0 / math.pi) * (x + 0.044715 * (x * x * x))))


def _log_sigmoid(x):
    return jnp.minimum(x, 0.0) - jnp.log1p(jnp.exp(-jnp.abs(x)))


def _norm_mod(x, g, shift, scale):
    ms = jnp.mean(x * x, axis=-1, keepdims=True)
    y = x * lax.rsqrt(ms + EPS) * g
    return y * (1.0 + scale) + shift


def _layer_norm(x, g, b):
    mu = jnp.mean(x, axis=-1, keepdims=True)
    xc = x - mu
    var = jnp.mean(xc * xc, axis=-1, keepdims=True)
    return xc * lax.rsqrt(var + EPS) * g + b


def _split3(x):
    hi = x.astype(BF16)
    r = x - hi.astype(F32)
    mid = r.astype(BF16)
    lo = (r - mid.astype(F32)).astype(BF16)
    return hi, mid, lo


def _ada_kernel(c_ref, w_ref, b_ref, o_ref):
    c = c_ref[...]
    cond = _silu(c).astype(BF16)
    o_ref[0] = _dot(cond, w_ref[0].astype(BF16)) + b_ref[0]


def _ada_call(c, ada_w, ada_b):
    n_layers, d, nm = ada_w.shape
    b = c.shape[0]
    bp = 16
    tn = nm // 8
    c_pad = jnp.zeros((bp, d), F32).at[:b].set(c)
    out = pl.pallas_call(
        _ada_kernel,
        grid=(n_layers, nm // tn),
        in_specs=[
            pl.BlockSpec((bp, d), lambda l, j: (0, 0)),
            pl.BlockSpec((1, d, tn), lambda l, j: (l, 0, j)),
            pl.BlockSpec((1, 1, tn), lambda l, j: (l, 0, j)),
        ],
        out_specs=pl.BlockSpec((1, bp, tn), lambda l, j: (l, 0, j)),
        out_shape=jax.ShapeDtypeStruct((n_layers, bp, nm), F32),
        compiler_params=pltpu.CompilerParams(
            dimension_semantics=("arbitrary", "arbitrary"), vmem_limit_bytes=VMEM_LIMIT),
        name="ada_mod",
    )(c_pad, ada_w, ada_b.reshape(n_layers, 1, nm))
    return out[:, :b].reshape(n_layers, b, N_MOD, d)


def _ffn_kernel(*refs, with_mix, mod_row, d_ff):
    if with_mix:
        (h_ref, mod_ref, g_ref, oa_ref, ob_ref, oc_ref, od_ref, wmix_ref,
         win_ref, wout_ref, out_ref, act_ref) = refs
    else:
        (h_ref, mod_ref, g_ref, win_ref, wout_ref, out_ref, act_ref) = refs
    x = h_ref[...]
    if with_mix:
        mix = _dot(oa_ref[...], wmix_ref[0 * GROUP_W:1 * GROUP_W, :])
        mix += _dot(ob_ref[...], wmix_ref[1 * GROUP_W:2 * GROUP_W, :])
        mix += _dot(oc_ref[...], wmix_ref[2 * GROUP_W:3 * GROUP_W, :])
        mix += _dot(od_ref[...], wmix_ref[3 * GROUP_W:4 * GROUP_W, :])
        x = x + mod_ref[0, mod_row - 1:mod_row, :] * mix
    hn = _norm_mod(x, g_ref[...], mod_ref[0, mod_row:mod_row + 1, :],
                   mod_ref[0, mod_row + 1:mod_row + 2, :]).astype(BF16)
    for c in range(d_ff // TF_FFN):
        cols = slice(c * TF_FFN, (c + 1) * TF_FFN)
        gate = _dot(hn, win_ref[:, cols])
        up = _dot(hn, win_ref[:, d_ff + c * TF_FFN:d_ff + (c + 1) * TF_FFN])
        act_ref[:, cols] = (_silu(gate) * up).astype(BF16)
    out_ref[...] = x + (0.5 * mod_ref[0, mod_row + 2:mod_row + 3, :]) * _dot(act_ref[...], wout_ref[...])


def _resident(shape):
    return pl.BlockSpec(shape, lambda i, _nd=len(shape): (0,) * _nd, pipeline_mode=pl.Buffered(1))


def _resident_layer(shape, layer):
    return pl.BlockSpec((None,) + tuple(shape[1:]), lambda i: (layer,) + (0,) * (len(shape) - 1),
                        pipeline_mode=pl.Buffered(1))


def _ffn_call(h, mod, g, w_in, w_out, *, layer, seq, mod_row, mix=None):
    n, d = h.shape
    d_ff = w_out.shape[1]
    tm = TM_FFN
    tiles_per_seq = seq // tm
    with_mix = mix is not None

    row = lambda i: (i, 0)
    in_specs = [
        pl.BlockSpec((tm, d), row),
        pl.BlockSpec((1, N_MOD, d), lambda i: (i // tiles_per_seq, 0, 0)),
        _resident((1, d)),
    ]
    args = [h, mod, g.reshape(1, d)]
    if with_mix:
        o_a, o_b, o_c, o_d, w_mix = mix
        in_specs += [pl.BlockSpec((tm, GROUP_W), row)] * 4 + [_resident_layer(w_mix.shape, layer)]
        args += [o_a, o_b, o_c, o_d, w_mix]
    in_specs += [_resident_layer(w_in.shape, layer), _resident_layer(w_out.shape, layer)]
    args += [w_in, w_out]
    return pl.pallas_call(
        functools.partial(_ffn_kernel, with_mix=with_mix, mod_row=mod_row, d_ff=d_ff),
        grid=(n // tm,),
        in_specs=in_specs,
        out_specs=pl.BlockSpec((tm, d), row),
        out_shape=jax.ShapeDtypeStruct((n, d), F32),
        scratch_shapes=[pltpu.VMEM((tm, d_ff), BF16)],
        compiler_params=pltpu.CompilerParams(
            dimension_semantics=("arbitrary",), vmem_limit_bytes=VMEM_LIMIT),
        name="ffn_mix" if with_mix else "ffn",
    )(*args)


_R_KA = 0
_R_CONV = _R_KA + GROUP_W
_R_GMLP = _R_CONV + 2 * GROUP_W
_R_KD = _R_GMLP + 2 * GROUP_W
_R_F = _R_KD + N_HEADS * FOX_PAD
_R_COLS = _R_F + 128
_T_QA = 0
_T_VA = _T_QA + GROUP_W
_T_QD = _T_VA + N_HEADS * V_ROWS
_T_VD = _T_QD + N_HEADS * FOX_PAD
_T_ROWS = _T_VD + N_HEADS * V_ROWS


def _proj_kernel(h_ref, mod_ref, g_ref, wrm_ref, wt_ref,
                 bd32_ref, ltri_ref, ek_ref, eq_ref,
                 gka_ref, gqa_ref, gkd_ref, gqd_ref, onek_ref, oneq_ref, fb_ref,
                 cw_ref, cb_ref, cng_ref, cnb_ref,
                 gng_ref, gnb_ref, gws_ref, gbs_ref,
                 ka_ref, qa_ref, va_ref, kd_ref, qd_ref, vd_ref, ob_ref, oc_ref,
                 ybuf_ref, carry_ref, *, steps_per_seq):
    tm = h_ref.shape[0] // PROJ_SUB
    i = pl.program_id(0)
    consts = (bd32_ref, ltri_ref, ek_ref, eq_ref,
              gka_ref, gqa_ref, gkd_ref, gqd_ref, onek_ref, oneq_ref, fb_ref,
              cw_ref, cb_ref, cng_ref, cnb_ref, gng_ref, gnb_ref, gws_ref, gbs_ref)
    outs = (ka_ref, qa_ref, va_ref, kd_ref, qd_ref, vd_ref, ob_ref, oc_ref)

    @pl.when(i % steps_per_seq == 0)
    def _():
        ybuf_ref[0:CONV_HALO, :] = jnp.zeros((CONV_HALO, GROUP_W), F32)
        ybuf_ref[CONV_HALO + tm:CONV_HALO + tm + CONV_TAIL, :] = jnp.zeros((CONV_TAIL, GROUP_W), F32)
        carry_ref[...] = jnp.zeros_like(carry_ref)

    projected = []
    for sub in range(PROJ_SUB):
        hn = _norm_mod(h_ref[sub * tm:(sub + 1) * tm, :], g_ref[...],
                       mod_ref[0, 3:4, :], mod_ref[0, 4:5, :]).astype(BF16)
        projected.append((_dot(hn, wrm_ref[...]),
                          _dot_nt(wt_ref[...], hn)))
    for sub, (zr_all, zt_all) in enumerate(projected):
        _proj_post(sub, zr_all, zt_all, consts, outs, ybuf_ref, carry_ref)


def _proj_post(sub, zr_all, zt_all, consts, outs, ybuf_ref, carry_ref):
    (bd32_ref, ltri_ref, ek_ref, eq_ref,
     gka_ref, gqa_ref, gkd_ref, gqd_ref, onek_ref, oneq_ref, fb_ref,
     cw_ref, cb_ref, cng_ref, cnb_ref, gng_ref, gnb_ref, gws_ref, gbs_ref) = consts
    ka_ref, qa_ref, va_ref, kd_ref, qd_ref, vd_ref, ob_ref, oc_ref = outs
    tm = zr_all.shape[0]
    r0 = sub * tm
    zr = lambda col, width: zr_all[:, col:col + width]
    zt = lambda row, height: zt_all[row:row + height, :]
    carry = carry_ref[0:1, :]

    zc = zr(_R_CONV, 2 * GROUP_W)
    y = zc[:, :GROUP_W] * _sigmoid(zc[:, GROUP_W:])
    ybuf_ref[CONV_HALO:CONV_HALO + tm, :] = y
    off = CONV_HALO - (CONV_KERNEL - 1)
    conv = jnp.zeros((tm, GROUP_W), F32) + cb_ref[...]
    for r in range(8):
        z = None
        for q in range((off + CONV_KERNEL - 1 - r) // 8 + 1):
            t = 8 * q + r - off
            if t < 0:
                continue
            term = cw_ref[t:t + 1, :] * ybuf_ref[8 * q:8 * q + tm + 8, :]
            z = term if z is None else z + term
        conv += z[r:r + tm, :]
    ybuf_ref[0:CONV_HALO, :] = ybuf_ref[tm:tm + CONV_HALO, :]
    ob_ref[r0:r0 + tm, :] = _silu(_layer_norm(conv, cng_ref[...], cnb_ref[...])).astype(BF16)

    gz = _gelu_tanh(zr(_R_GMLP, 2 * GROUP_W))
    u = gz[:, :GROUP_W]
    vn = _layer_norm(gz[:, GROUP_W:], gng_ref[...], gnb_ref[...]).astype(BF16)
    tri = (lax.broadcasted_iota(jnp.int32, (N_HEADS * CHUNK, CHUNK), 0) % CHUNK
           >= lax.broadcasted_iota(jnp.int32, (N_HEADS * CHUNK, CHUNK), 1))
    ws = jnp.where(tri, gws_ref[...], 0.0).astype(BF16)
    lane_head = lax.broadcasted_iota(jnp.int32, (CHUNK, GROUP_W), 1) // HEAD_DIM
    for ci in range(tm // CHUNK):
        rows = slice(ci * CHUNK, (ci + 1) * CHUNK)
        res = _dot(ws, vn[rows, :])
        s = res[0:CHUNK, :]
        for hd in range(1, N_HEADS):
            s = jnp.where(lane_head == hd, res[hd * CHUNK:(hd + 1) * CHUNK, :], s)
        oc_ref[r0 + ci * CHUNK:r0 + (ci + 1) * CHUNK, :] = (u[rows, :] * (s + gbs_ref[...])).astype(BF16)

    head_lane = lax.broadcasted_iota(jnp.int32, (tm, 128), 1) < N_HEADS

    def pack3(x):
        hi, mid, lo = _split3(jnp.where(head_lane, x, 0.0))
        return (hi.astype(F32) + pltpu.roll(mid.astype(F32), N_HEADS, axis=1)
                + pltpu.roll(lo.astype(F32), 2 * N_HEADS, axis=1)).astype(BF16)

    lf = _log_sigmoid(zr(_R_F, 128) + fb_ref[...]) * LOG2E
    c3 = _dot(ltri_ref[...], pack3(lf))
    cum = (c3 + pltpu.roll(c3, 128 - N_HEADS, axis=1)
           + pltpu.roll(c3, 128 - 2 * N_HEADS, axis=1)) + carry
    carry_ref[...] = jnp.broadcast_to(cum[tm - 1:tm, :], carry_ref.shape)
    cpk = pack3(cum)
    aug_k = _dot(cpk, ek_ref[...]) + onek_ref[...]
    aug_q = _dot_nt(eq_ref[...], cpk) + oneq_ref[...]
    zk = zr(_R_KD, N_HEADS * FOX_PAD)
    zq = zt(_T_QD, N_HEADS * FOX_PAD)
    for hd in range(N_HEADS):
        hs = slice(hd * FOX_PAD, (hd + 1) * FOX_PAD)
        blk = zk[:, hs]
        ss = jnp.sum(blk * blk, axis=-1, keepdims=True)
        kd_ref[r0:r0 + tm, hs] = (blk * lax.rsqrt(ss * (1.0 / HEAD_DIM) + EPS) * gkd_ref[:, hs]
                                  + aug_k[:, hs]).astype(BF16)
        blk = zq[hs, :]
        ss = jnp.sum(blk * blk, axis=0, keepdims=True)
        qd = (blk * lax.rsqrt(ss * (1.0 / HEAD_DIM) + EPS) * gqd_ref[hs, :] + aug_q[hs, :]).astype(BF16)
        for cb in range(tm // TQ):
            qd_ref[sub * (tm // TQ) + cb, hs, :] = qd[:, cb * TQ:(cb + 1) * TQ]

    zk = zr(_R_KA, GROUP_W)
    ss = _dot((zk * zk).astype(BF16), bd32_ref[...])
    ka_ref[r0:r0 + tm, :] = (zk * lax.rsqrt(ss * (1.0 / DIFF_QK_DIM) + EPS) * gka_ref[...]).astype(BF16)
    zq = zt(_T_QA, GROUP_W)
    ss = _dot(bd32_ref[...], (zq * zq).astype(BF16))
    qa = (zq * lax.rsqrt(ss * (1.0 / DIFF_QK_DIM) + EPS) * gqa_ref[...]).astype(BF16)
    for cb in range(tm // TQ):
        qa_ref[sub * (tm // TQ) + cb] = qa[:, cb * TQ:(cb + 1) * TQ]

    vrow = lax.broadcasted_iota(jnp.int32, (N_HEADS * V_ROWS, tm), 0) % V_ROWS
    ones_rows = (vrow >= HEAD_DIM).astype(F32)
    va = (zt(_T_VA, N_HEADS * V_ROWS) + ones_rows).astype(BF16)
    vd = (zt(_T_VD, N_HEADS * V_ROWS) + ones_rows).astype(BF16)
    for cb in range(tm // TK):
        va_ref[sub * (tm // TK) + cb] = va[:, cb * TK:(cb + 1) * TK]
        vd_ref[sub * (tm // TK) + cb] = vd[:, cb * TK:(cb + 1) * TK]


def _proj_constants(tm):
    def block_diag(n, blk):
        idx = np.arange(n) // blk
        return (idx[:, None] == idx[None, :]).astype(np.float32)

    ltri = np.tril(np.ones((tm, tm), np.float32))
    ek = np.zeros((128, N_HEADS * FOX_PAD), np.float32)
    eq = np.zeros((N_HEADS * FOX_PAD, 128), np.float32)
    onek = np.zeros((1, N_HEADS * FOX_PAD), np.float32)
    oneq = np.zeros((N_HEADS * FOX_PAD, 1), np.float32)
    for hd in range(N_HEADS):
        base = hd * FOX_PAD + HEAD_DIM
        for p in range(3):
            ek[p * N_HEADS + hd, base + p] = -1.0
            eq[base + 3 + p, p * N_HEADS + hd] = 1.0
        onek[0, base + 3:base + 6] = 1.0
        oneq[base:base + 3, 0] = 1.0
    return dict(
        bd32=jnp.asarray(block_diag(GROUP_W, DIFF_QK_DIM), BF16),
        ltri=jnp.asarray(ltri, BF16), ek=jnp.asarray(ek, BF16), eq=jnp.asarray(eq, BF16),
        onek=jnp.asarray(onek), oneq=jnp.asarray(oneq))


def _proj_weights(w_in):
    d = w_in.shape[0]
    c0 = 0
    qa = w_in[:, c0:c0 + GROUP_W]; c0 += GROUP_W
    ka = w_in[:, c0:c0 + GROUP_W]; c0 += GROUP_W
    va = w_in[:, c0:c0 + GROUP_W]; c0 += GROUP_W
    conv = w_in[:, c0:c0 + 2 * GROUP_W]; c0 += 2 * GROUP_W
    gmlp = w_in[:, c0:c0 + 2 * GROUP_W]; c0 += 2 * GROUP_W
    qd = w_in[:, c0:c0 + GROUP_W]; c0 += GROUP_W
    kd = w_in[:, c0:c0 + GROUP_W]; c0 += GROUP_W
    vd = w_in[:, c0:c0 + GROUP_W]; c0 += GROUP_W
    wf = w_in[:, c0:c0 + N_HEADS]

    def pad_heads(w, width):
        w = w.reshape(d, N_HEADS, HEAD_DIM)
        w = jnp.pad(w, ((0, 0), (0, 0), (0, width - HEAD_DIM)))
        return w.reshape(d, N_HEADS * width)

    w_rm = jnp.concatenate(
        [ka, conv, gmlp, pad_heads(kd, FOX_PAD), jnp.pad(wf, ((0, 0), (0, 128 - N_HEADS)))], axis=1)
    w_t = jnp.concatenate(
        [qa, pad_heads(va, V_ROWS), pad_heads(qd, FOX_PAD), pad_heads(vd, V_ROWS)], axis=1).T
    return w_rm.astype(BF16), w_t.astype(BF16)


def _proj_call(h, mod, g, w_in, consts, diff_qk_g, fox_qk_g, fox_fb,
               conv_w, conv_b, conv_ng, conv_nb, gmlp_ng, gmlp_nb, gmlp_ws, gmlp_bs, *, seq):
    n, d = h.shape
    sub_tm = TM_PROJ
    tm = PROJ_SUB * sub_tm
    tiles_per_seq = seq // tm
    nqb, nkb = tm // TQ, tm // TK
    w_rm, w_t = _proj_weights(w_in)

    diff_scale = DIFF_QK_DIM ** -0.5 * LOG2E
    fox_scale = HEAD_DIM ** -0.5 * LOG2E
    gka = jnp.tile(diff_qk_g[1], 2 * N_HEADS).reshape(1, GROUP_W)
    gqa = (jnp.tile(diff_qk_g[0], 2 * N_HEADS) * diff_scale).reshape(GROUP_W, 1)
    pad = jnp.zeros((FOX_PAD - HEAD_DIM,), F32)
    gkd = jnp.tile(jnp.concatenate([fox_qk_g[1], pad]), N_HEADS).reshape(1, N_HEADS * FOX_PAD)
    gqd = jnp.tile(jnp.concatenate([fox_qk_g[0] * fox_scale, pad]), N_HEADS).reshape(N_HEADS * FOX_PAD, 1)
    fb = jnp.pad(fox_fb, (0, 128 - N_HEADS)).reshape(1, 128)
    cw = jnp.pad(conv_w, ((0, 32 - CONV_KERNEL), (0, 0)))
    gws = gmlp_ws.reshape(N_HEADS * CHUNK, CHUNK)
    gbs = jnp.repeat(gmlp_bs.T, HEAD_DIM, axis=1)

    full = lambda a: _resident(a.shape)
    row = lambda i: (i, 0)
    small = [consts["bd32"], consts["ltri"], consts["ek"], consts["eq"],
             gka, gqa, gkd, gqd, consts["onek"], consts["oneq"], fb,
             cw, conv_b.reshape(1, -1), conv_ng.reshape(1, -1), conv_nb.reshape(1, -1),
             gmlp_ng.reshape(1, -1), gmlp_nb.reshape(1, -1), gws, gbs]
    in_specs = [
        pl.BlockSpec((tm, d), row),
        pl.BlockSpec((1, N_MOD, d), lambda i: (i // tiles_per_seq, 0, 0)),
        full(g.reshape(1, d)),
        full(w_rm), full(w_t),
    ] + [full(a) for a in small]
    blk3 = lambda i: (i, 0, 0)
    out_specs = [
        pl.BlockSpec((tm, GROUP_W), row),
        pl.BlockSpec((nqb, GROUP_W, TQ), blk3),
        pl.BlockSpec((nkb, N_HEADS * V_ROWS, TK), blk3),
        pl.BlockSpec((tm, N_HEADS * FOX_PAD), row),
        pl.BlockSpec((nqb, N_HEADS * FOX_PAD, TQ), blk3),
        pl.BlockSpec((nkb, N_HEADS * V_ROWS, TK), blk3),
        pl.BlockSpec((tm, GROUP_W), row),
        pl.BlockSpec((tm, GROUP_W), row),
    ]
    out_shape = [
        jax.ShapeDtypeStruct((n, GROUP_W), BF16),
        jax.ShapeDtypeStruct((n // TQ, GROUP_W, TQ), BF16),
        jax.ShapeDtypeStruct((n // TK, N_HEADS * V_ROWS, TK), BF16),
        jax.ShapeDtypeStruct((n, N_HEADS * FOX_PAD), BF16),
        jax.ShapeDtypeStruct((n // TQ, N_HEADS * FOX_PAD, TQ), BF16),
        jax.ShapeDtypeStruct((n // TK, N_HEADS * V_ROWS, TK), BF16),
        jax.ShapeDtypeStruct((n, GROUP_W), BF16),
        jax.ShapeDtypeStruct((n, GROUP_W), BF16),
    ]
    return pl.pallas_call(
        functools.partial(_proj_kernel, steps_per_seq=tiles_per_seq),
        grid=(n // tm,),
        in_specs=in_specs,
        out_specs=out_specs,
        out_shape=out_shape,
        scratch_shapes=[pltpu.VMEM((CONV_HALO + sub_tm + CONV_TAIL, GROUP_W), F32),
                        pltpu.VMEM((8, 128), F32)],
        compiler_params=pltpu.CompilerParams(
            dimension_semantics=("arbitrary",), vmem_limit_bytes=VMEM_LIMIT),
        name="in_proj",
    )(h, mod, g.reshape(1, d), w_rm, w_t, *small)


def _attend(n_chains, q_of, k_of, v_of, s_refs, m_ref, acc_ref, i):
    m_ref[...] = jnp.full(m_ref.shape, NEG_INF, F32)
    acc_ref[...] = jnp.zeros(acc_ref.shape, F32)
    all_q, upper_q = slice(0, TQ), slice(TK, TQ)

    def scores(j, s_ref, ch, qs):
        s_ref[ch, :, qs] = _dot(k_of(ch, j), q_of(ch)[:, qs])

    def update(j, s_ref, ch, masked, qs):
        width = qs.stop - qs.start
        s = s_ref[ch, :, qs]
        if masked:
            kp = lax.broadcasted_iota(jnp.int32, (TK, width), 0) + (j * TK - i * TQ - qs.start)
            s = jnp.where(kp <= lax.broadcasted_iota(jnp.int32, (TK, width), 1), s, NEG_INF)
        m_full = m_ref[ch]
        m_old = m_full[:, qs]
        m_new = jnp.maximum(m_old, jnp.max(s, axis=0, keepdims=True))
        p = jnp.exp2(s - m_new).astype(BF16)
        acc_ref[ch, :, qs] = acc_ref[ch, :, qs] * jnp.exp2(m_old - m_new) + _dot(v_of(ch, j), p)
        m_ref[ch] = m_new if width == TQ else jnp.concatenate([m_full[:, :qs.start], m_new], axis=1)

    def step(j_next, s_next, j_cur, s_cur, masked, qs_next=all_q, qs_cur=all_q):
        for ch in range(n_chains):
            if j_next is not None:
                scores(j_next, s_next, ch, qs_next)
            update(j_cur, s_cur, ch, masked, qs_cur)

    n_pairs = i
    for ch in range(n_chains):
        scores(0, s_refs[0], ch, all_q)

    def body(jj, carry):
        step(2 * jj + 1, s_refs[1], 2 * jj, s_refs[0], False)
        step(2 * jj + 2, s_refs[0], 2 * jj + 1, s_refs[1], False)
        return carry

    lax.fori_loop(0, n_pairs, body, 0)
    step(2 * n_pairs + 1, s_refs[1], 2 * n_pairs, s_refs[0], True, qs_next=upper_q)
    step(None, None, 2 * n_pairs + 1, s_refs[1], True, qs_cur=upper_q)


def _normalised(acc_ref, ch):
    return acc_ref[ch, 0:HEAD_DIM, :] * (1.0 / acc_ref[ch, HEAD_DIM:HEAD_DIM + 1, :])


def _diff_attn_kernel(q_ref, k_ref, v_ref, lam_ref, og_ref, o_ref,
                      qm_ref, s0_ref, s1_ref, m_ref, acc_ref, ot_ref, *, lam_init):
    i = pl.program_id(1)
    q_t = q_ref[0]
    rows = lax.broadcasted_iota(jnp.int32, (GROUP_W, TQ), 0)
    for ch in range(2 * N_HEADS):
        lo = ch * DIFF_QK_DIM
        qm_ref[ch] = jnp.where((rows >= lo) & (rows < lo + DIFF_QK_DIM), q_t, jnp.zeros_like(q_t))
    _attend(2 * N_HEADS, lambda ch: qm_ref[ch], lambda ch, j: k_ref[j],
            lambda ch, j: v_ref[j, (ch // 2) * V_ROWS:(ch // 2 + 1) * V_ROWS, :],
            (s0_ref, s1_ref), m_ref, acc_ref, i)
    lv = lam_ref[...]
    lam = (jnp.exp(jnp.sum(lv[0:1, :] * lv[1:2, :], axis=-1, keepdims=True))
           - jnp.exp(jnp.sum(lv[2:3, :] * lv[3:4, :], axis=-1, keepdims=True)) + lam_init)
    for hd in range(N_HEADS):
        o = _normalised(acc_ref, 2 * hd) - lam * _normalised(acc_ref, 2 * hd + 1)
        ms = jnp.mean(o * o, axis=0, keepdims=True)
        ot_ref[hd * HEAD_DIM:(hd + 1) * HEAD_DIM, :] = o * lax.rsqrt(ms + EPS) * og_ref[...]
    o_ref[...] = ot_ref[...].T.astype(BF16)


def _fox_attn_kernel(q_ref, k_ref, v_ref, o_ref, s0_ref, s1_ref, m_ref, acc_ref, ot_ref):
    i = pl.program_id(1)
    _attend(N_HEADS, lambda ch: q_ref[0, ch * FOX_PAD:(ch + 1) * FOX_PAD, :],
            lambda ch, j: k_ref[j, :, ch * FOX_PAD:(ch + 1) * FOX_PAD],
            lambda ch, j: v_ref[j, ch * V_ROWS:(ch + 1) * V_ROWS, :],
            (s0_ref, s1_ref), m_ref, acc_ref, i)
    for hd in range(N_HEADS):
        ot_ref[hd * HEAD_DIM:(hd + 1) * HEAD_DIM, :] = _normalised(acc_ref, hd)
    o_ref[...] = ot_ref[...].T.astype(BF16)


def _attn_call(kernel, q_t, k, v_t, extra, *, n_chains, masked_q, batch, seq, name):
    nq, nk = seq // TQ, seq // TK
    feat = q_t.shape[1]
    k = k.reshape(batch * nk, TK, feat)
    scratch = [pltpu.VMEM((n_chains, feat, TQ), BF16)] if masked_q else []
    scratch += [pltpu.VMEM((n_chains, TK, TQ), F32), pltpu.VMEM((n_chains, TK, TQ), F32),
                pltpu.VMEM((n_chains, 1, TQ), F32), pltpu.VMEM((n_chains, V_ROWS, TQ), F32),
                pltpu.VMEM((GROUP_W, TQ), F32)]
    in_specs = [
        pl.BlockSpec((1, feat, TQ), lambda b, i: (b * nq + i, 0, 0)),
        pl.BlockSpec((nk, TK, feat), lambda b, i: (b, 0, 0)),
        pl.BlockSpec((nk, N_HEADS * V_ROWS, TK), lambda b, i: (b, 0, 0)),
    ] + [pl.BlockSpec(a.shape, lambda b, i, _nd=a.ndim: (0,) * _nd) for a in extra]
    return pl.pallas_call(
        kernel,
        grid=(batch, nq),
        in_specs=in_specs,
        out_specs=pl.BlockSpec((TQ, GROUP_W), lambda b, i: (b * nq + i, 0)),
        out_shape=jax.ShapeDtypeStruct((batch * seq, GROUP_W), BF16),
        scratch_shapes=scratch,
        compiler_params=pltpu.CompilerParams(
            dimension_semantics=("arbitrary", "arbitrary"), vmem_limit_bytes=VMEM_LIMIT),
        name=name,
    )(q_t, k, v_t, *extra)


def kernel(x, c, ada_w, ada_b, norm_g, ffn1_w_in, ffn1_w_out, ffn2_w_in, ffn2_w_out, w_in, w_out, diff_qk_g, diff_lambda, diff_out_g, conv_w, conv_b, conv_norm_g, conv_norm_b, gmlp_norm_g, gmlp_norm_b, gmlp_ws, gmlp_bs, fox_qk_g, fox_fb):
    batch, seq, d = x.shape
    depth = ada_w.shape[0]
    assert seq % (PROJ_SUB * TM_PROJ) == 0 and seq % TM_FFN == 0 and TM_PROJ % TQ == 0
    assert w_in.shape[-1] == 10 * GROUP_W + N_HEADS and w_out.shape[1] == 4 * GROUP_W
    assert ffn1_w_out.shape[1] % TF_FFN == 0

    mods = _ada_call(c, ada_w, ada_b)
    w1_in, w1_out, w2_in, w2_out, w_mix = (
        a.astype(BF16) for a in (ffn1_w_in, ffn1_w_out, ffn2_w_in, ffn2_w_out, w_out))
    consts = _proj_constants(TM_PROJ)
    h = x.reshape(batch * seq, d)
    for l in range(depth):
        mod = mods[l]
        h = _ffn_call(h, mod, norm_g[l, 0], w1_in, w1_out, layer=l, seq=seq, mod_row=0)
        ka, qa_t, va_t, kd, qd_t, vd_t, o_b, o_c = _proj_call(
            h, mod, norm_g[l, 1], w_in[l], consts, diff_qk_g[l], fox_qk_g[l], fox_fb[l],
            conv_w[l], conv_b[l], conv_norm_g[l], conv_norm_b[l],
            gmlp_norm_g[l], gmlp_norm_b[l], gmlp_ws[l], gmlp_bs[l], seq=seq)
        lam_init = 0.8 - 0.6 * math.exp(-0.3 * l)
        og = (diff_out_g[l] * (1.0 - lam_init)).reshape(HEAD_DIM, 1)
        o_a = _attn_call(functools.partial(_diff_attn_kernel, lam_init=lam_init),
                         qa_t, ka, va_t, [diff_lambda[l], og], n_chains=2 * N_HEADS, masked_q=True,
                         batch=batch, seq=seq, name="diff_attn")
        o_d = _attn_call(_fox_attn_kernel, qd_t, kd, vd_t, [], n_chains=N_HEADS, masked_q=False,
                         batch=batch, seq=seq, name="fox_attn")
        h = _ffn_call(h, mod, norm_g[l, 2], w2_in, w2_out, layer=l, seq=seq, mod_row=6,
                      mix=(o_a, o_b, o_c, o_d, w_mix))
    return h.reshape(batch, seq, d)
```

```python
import functools
import math

import numpy as np
import jax
import jax.numpy as jnp
from jax import lax
from jax.experimental import pallas as pl
from jax.experimental.pallas import tpu as pltpu

F32 = jnp.float32
BF16 = jnp.bfloat16

HEAD_DIM = 64
N_HEADS = 4
GROUP_W = N_HEADS * HEAD_DIM
DIFF_QK_DIM = HEAD_DIM // 2
CONV_KERNEL = 31
CHUNK = 128
N_MOD = 9
EPS = 1e-6
NEG_INF = -1e30
LOG2E = math.log2(math.e)

V_ROWS = HEAD_DIM + 16
FOX_PAD = 128
CONV_HALO = 32
CONV_TAIL = 16

TQ = 512
TK = TQ // 2
TM_PROJ = 512
PROJ_SUB = 2
TM_FFN = 512
TF_FFN = 256
VMEM_LIMIT = 56 * 1024 * 1024


def _dot(a, b):
    return jnp.dot(a, b, preferred_element_type=F32)


def _dot_nt(a, b):
    return lax.dot_general(a, b, (((1,), (1,)), ((), ())), preferred_element_type=F32)


def _sigmoid(x):
    return 1.0 / (1.0 + jnp.exp(-x))


def _silu(x):
    return x * _sigmoid(x)


def _gelu_tanh(x):
    return 0.5 * x * (1.0 + jnp.tanh(math.sqrt(2.0 / math.pi) * (x + 0.044715 * (x * x * x))))


def _log_sigmoid(x):
    return jnp.minimum(x, 0.0) - jnp.log1p(jnp.exp(-jnp.abs(x)))


def _norm_mod(x, g, shift, scale):
    ms = jnp.mean(x * x, axis=-1, keepdims=True)
    y = x * lax.rsqrt(ms + EPS) * g
    return y * (1.0 + scale) + shift


def _layer_norm(x, g, b):
    mu = jnp.mean(x, axis=-1, keepdims=True)
    xc = x - mu
    var = jnp.mean(xc * xc, axis=-1, keepdims=True)
    return xc * lax.rsqrt(var + EPS) * g + b


def _split3(x):
    hi = x.astype(BF16)
    r = x - hi.astype(F32)
    mid = r.astype(BF16)
    lo = (r - mid.astype(F32)).astype(BF16)
    return hi, mid, lo


def _ada_kernel(c_ref, w_ref, b_ref, o_ref):
    c = c_ref[...]
    cond = _silu(c).astype(BF16)
    o_ref[0] = _dot(cond, w_ref[0].astype(BF16)) + b_ref[0]


def _ada_call(c, ada_w, ada_b):
    n_layers, d, nm = ada_w.shape
    b = c.shape[0]
    bp = 16
    tn = nm // 8
    c_pad = jnp.zeros((bp, d), F32).at[:b].set(c)
    out = pl.pallas_call(
        _ada_kernel,
        grid=(n_layers, nm // tn),
        in_specs=[
            pl.BlockSpec((bp, d), lambda l, j: (0, 0)),
            pl.BlockSpec((1, d, tn), lambda l, j: (l, 0, j)),
            pl.BlockSpec((1, 1, tn), lambda l, j: (l, 0, j)),
        ],
        out_specs=pl.BlockSpec((1, bp, tn), lambda l, j: (l, 0, j)),
        out_shape=jax.ShapeDtypeStruct((n_layers, bp, nm), F32),
        compiler_params=pltpu.CompilerParams(
            dimension_semantics=("arbitrary", "arbitrary"), vmem_limit_bytes=VMEM_LIMIT),
        name="ada_mod",
    )(c_pad, ada_w, ada_b.reshape(n_layers, 1, nm))
    return out[:, :b].reshape(n_layers, b, N_MOD, d)


def _ffn_kernel(*refs, with_mix, mod_row, d_ff):
    if with_mix:
        (h_ref, mod_ref, g_ref, oa_ref, ob_ref, oc_ref, od_ref, wmix_ref,
         win_ref, wout_ref, out_ref, act_ref) = refs
    else:
        (h_ref, mod_ref, g_ref, win_ref, wout_ref, out_ref, act_ref) = refs
    x = h_ref[...]
    if with_mix:
        mix = _dot(oa_ref[...], wmix_ref[0 * GROUP_W:1 * GROUP_W, :])
        mix += _dot(ob_ref[...], wmix_ref[1 * GROUP_W:2 * GROUP_W, :])
        mix += _dot(oc_ref[...], wmix_ref[2 * GROUP_W:3 * GROUP_W, :])
        mix += _dot(od_ref[...], wmix_ref[3 * GROUP_W:4 * GROUP_W, :])
        x = x + mod_ref[0, mod_row - 1:mod_row, :] * mix
    hn = _norm_mod(x, g_ref[...], mod_ref[0, mod_row:mod_row + 1, :],
                   mod_ref[0, mod_row + 1:mod_row + 2, :]).astype(BF16)
    for c in range(d_ff // TF_FFN):
        cols = slice(c * TF_FFN, (c + 1) * TF_FFN)
        gate = _dot(hn, win_ref[:, cols])
        up = _dot(hn, win_ref[:, d_ff + c * TF_FFN:d_ff + (c + 1) * TF_FFN])
        act_ref[:, cols] = (_silu(gate) * up).astype(BF16)
    out_ref[...] = x + (0.5 * mod_ref[0, mod_row + 2:mod_row + 3, :]) * _dot(act_ref[...], wout_ref[...])


def _resident(shape):
    return pl.BlockSpec(shape, lambda i, _nd=len(shape): (0,) * _nd, pipeline_mode=pl.Buffered(1))


def _resident_layer(shape, layer):
    return pl.BlockSpec((None,) + tuple(shape[1:]), lambda i: (layer,) + (0,) * (len(shape) - 1),
                        pipeline_mode=pl.Buffered(1))


def _ffn_call(h, mod, g, w_in, w_out, *, layer, seq, mod_row, mix=None):
    n, d = h.shape
    d_ff = w_out.shape[1]
    tm = TM_FFN
    tiles_per_seq = seq // tm
    with_mix = mix is not None

    row = lambda i: (i, 0)
    in_specs = [
        pl.BlockSpec((tm, d), row),
        pl.BlockSpec((1, N_MOD, d), lambda i: (i // tiles_per_seq, 0, 0)),
        _resident((1, d)),
    ]
    args = [h, mod, g.reshape(1, d)]
    if with_mix:
        o_a, o_b, o_c, o_d, w_mix = mix
        in_specs += [pl.BlockSpec((tm, GROUP_W), row)] * 4 + [_resident_layer(w_mix.shape, layer)]
        args += [o_a, o_b, o_c, o_d, w_mix]
    in_specs += [_resident_layer(w_in.shape, layer), _resident_layer(w_out.shape, layer)]
    args += [w_in, w_out]
    return pl.pallas_call(
        functools.partial(_ffn_kernel, with_mix=with_mix, mod_row=mod_row, d_ff=d_ff),
        grid=(n // tm,),
        in_specs=in_specs,
        out_specs=pl.BlockSpec((tm, d), row),
        out_shape=jax.ShapeDtypeStruct((n, d), F32),
        scratch_shapes=[pltpu.VMEM((tm, d_ff), BF16)],
        compiler_params=pltpu.CompilerParams(
            dimension_semantics=("arbitrary",), vmem_limit_bytes=VMEM_LIMIT),
        name="ffn_mix" if with_mix else "ffn",
    )(*args)


_R_KA = 0
_R_CONV = _R_KA + GROUP_W
_R_GMLP = _R_CONV + 2 * GROUP_W
_R_KD = _R_GMLP + 2 * GROUP_W
_R_F = _R_KD + N_HEADS * FOX_PAD
_R_COLS = _R_F + 128
_T_QA = 0
_T_VA = _T_QA + GROUP_W
_T_QD = _T_VA + N_HEADS * V_ROWS
_T_VD = _T_QD + N_HEADS * FOX_PAD
_T_ROWS = _T_VD + N_HEADS * V_ROWS


def _proj_kernel(h_ref, mod_ref, g_ref, wrm_ref, wt_ref,
                 bd32_ref, ltri_ref, ek_ref, eq_ref,
                 gka_ref, gqa_ref, gkd_ref, gqd_ref, onek_ref, oneq_ref, fb_ref,
                 cw_ref, cb_ref, cng_ref, cnb_ref,
                 gng_ref, gnb_ref, gws_ref, gbs_ref,
                 ka_ref, qa_ref, va_ref, kd_ref, qd_ref, vd_ref, ob_ref, oc_ref,
                 ybuf_ref, carry_ref, *, steps_per_seq):
    tm = h_ref.shape[0] // PROJ_SUB
    i = pl.program_id(0)
    consts = (bd32_ref, ltri_ref, ek_ref, eq_ref,
              gka_ref, gqa_ref, gkd_ref, gqd_ref, onek_ref, oneq_ref, fb_ref,
              cw_ref, cb_ref, cng_ref, cnb_ref, gng_ref, gnb_ref, gws_ref, gbs_ref)
    outs = (ka_ref, qa_ref, va_ref, kd_ref, qd_ref, vd_ref, ob_ref, oc_ref)

    @pl.when(i % steps_per_seq == 0)
    def _():
        ybuf_ref[0:CONV_HALO, :] = jnp.zeros((CONV_HALO, GROUP_W), F32)
        ybuf_ref[CONV_HALO + tm:CONV_HALO + tm + CONV_TAIL, :] = jnp.zeros((CONV_TAIL, GROUP_W), F32)
        carry_ref[...] = jnp.zeros_like(carry_ref)

    projected = []
    for sub in range(PROJ_SUB):
        hn = _norm_mod(h_ref[sub * tm:(sub + 1) * tm, :], g_ref[...],
                       mod_ref[0, 3:4, :], mod_ref[0, 4:5, :]).astype(BF16)
        projected.append((_dot(hn, wrm_ref[...]),
                          _dot_nt(wt_ref[...], hn)))
    for sub, (zr_all, zt_all) in enumerate(projected):
        _proj_post(sub, zr_all, zt_all, consts, outs, ybuf_ref, carry_ref)


def _proj_post(sub, zr_all, zt_all, consts, outs, ybuf_ref, carry_ref):
    (bd32_ref, ltri_ref, ek_ref, eq_ref,
     gka_ref, gqa_ref, gkd_ref, gqd_ref, onek_ref, oneq_ref, fb_ref,
     cw_ref, cb_ref, cng_ref, cnb_ref, gng_ref, gnb_ref, gws_ref, gbs_ref) = consts
    ka_ref, qa_ref, va_ref, kd_ref, qd_ref, vd_ref, ob_ref, oc_ref = outs
    tm = zr_all.shape[0]
    r0 = sub * tm
    zr = lambda col, width: zr_all[:, col:col + width]
    zt = lambda row, height: zt_all[row:row + height, :]
    carry = carry_ref[0:1, :]

    zc = zr(_R_CONV, 2 * GROUP_W)
    y = zc[:, :GROUP_W] * _sigmoid(zc[:, GROUP_W:])
    ybuf_ref[CONV_HALO:CONV_HALO + tm, :] = y
    off = CONV_HALO - (CONV_KERNEL - 1)
    conv = jnp.zeros((tm, GROUP_W), F32) + cb_ref[...]
    for r in range(8):
        z = None
        for q in range((off + CONV_KERNEL - 1 - r) // 8 + 1):
            t = 8 * q + r - off
            if t < 0:
                continue
            term = cw_ref[t:t + 1, :] * ybuf_ref[8 * q:8 * q + tm + 8, :]
            z = term if z is None else z + term
        conv += z[r:r + tm, :]
    ybuf_ref[0:CONV_HALO, :] = ybuf_ref[tm:tm + CONV_HALO, :]
    ob_ref[r0:r0 + tm, :] = _silu(_layer_norm(conv, cng_ref[...], cnb_ref[...])).astype(BF16)

    gz = _gelu_tanh(zr(_R_GMLP, 2 * GROUP_W))
    u = gz[:, :GROUP_W]
    vn = _layer_norm(gz[:, GROUP_W:], gng_ref[...], gnb_ref[...]).astype(BF16)
    tri = (lax.broadcasted_iota(jnp.int32, (N_HEADS * CHUNK, CHUNK), 0) % CHUNK
           >= lax.broadcasted_iota(jnp.int32, (N_HEADS * CHUNK, CHUNK), 1))
    ws = jnp.where(tri, gws_ref[...], 0.0).astype(BF16)
    lane_head = lax.broadcasted_iota(jnp.int32, (CHUNK, GROUP_W), 1) // HEAD_DIM
    for ci in range(tm // CHUNK):
        rows = slice(ci * CHUNK, (ci + 1) * CHUNK)
        res = _dot(ws, vn[rows, :])
        s = res[0:CHUNK, :]
        for hd in range(1, N_HEADS):
            s = jnp.where(lane_head == hd, res[hd * CHUNK:(hd + 1) * CHUNK, :], s)
        oc_ref[r0 + ci * CHUNK:r0 + (ci + 1) * CHUNK, :] = (u[rows, :] * (s + gbs_ref[...])).astype(BF16)

    head_lane = lax.broadcasted_iota(jnp.int32, (tm, 128), 1) < N_HEADS

    def pack3(x):
        hi, mid, lo = _split3(jnp.where(head_lane, x, 0.0))
        return (hi.astype(F32) + pltpu.roll(mid.astype(F32), N_HEADS, axis=1)
                + pltpu.roll(lo.astype(F32), 2 * N_HEADS, axis=1)).astype(BF16)

    lf = _log_sigmoid(zr(_R_F, 128) + fb_ref[...]) * LOG2E
    c3 = _dot(ltri_ref[...], pack3(lf))
    cum = (c3 + pltpu.roll(c3, 128 - N_HEADS, axis=1)
           + pltpu.roll(c3, 128 - 2 * N_HEADS, axis=1)) + carry
    carry_ref[...] = jnp.broadcast_to(cum[tm - 1:tm, :], carry_ref.shape)
    cpk = pack3(cum)
    aug_k = _dot(cpk, ek_ref[...]) + onek_ref[...]
    aug_q = _dot_nt(eq_ref[...], cpk) + oneq_ref[...]
    zk = zr(_R_KD, N_HEADS * FOX_PAD)
    zq = zt(_T_QD, N_HEADS * FOX_PAD)
    for hd in range(N_HEADS):
        hs = slice(hd * FOX_PAD, (hd + 1) * FOX_PAD)
        blk = zk[:, hs]
        ss = jnp.sum(blk * blk, axis=-1, keepdims=True)
        kd_ref[r0:r0 + tm, hs] = (blk * lax.rsqrt(ss * (1.0 / HEAD_DIM) + EPS) * gkd_ref[:, hs]
                                  + aug_k[:, hs]).astype(BF16)
        blk = zq[hs, :]
        ss = jnp.sum(blk * blk, axis=0, keepdims=True)
        qd = (blk * lax.rsqrt(ss * (1.0 / HEAD_DIM) + EPS) * gqd_ref[hs, :] + aug_q[hs, :]).astype(BF16)
        for cb in range(tm // TQ):
            qd_ref[sub * (tm // TQ) + cb, hs, :] = qd[:, cb * TQ:(cb + 1) * TQ]

    zk = zr(_R_KA, GROUP_W)
    ss = _dot((zk * zk).astype(BF16), bd32_ref[...])
    ka_ref[r0:r0 + tm, :] = (zk * lax.rsqrt(ss * (1.0 / DIFF_QK_DIM) + EPS) * gka_ref[...]).astype(BF16)
    zq = zt(_T_QA, GROUP_W)
    ss = _dot(bd32_ref[...], (zq * zq).astype(BF16))
    qa = (zq * lax.rsqrt(ss * (1.0 / DIFF_QK_DIM) + EPS) * gqa_ref[...]).astype(BF16)
    for cb in range(tm // TQ):
        qa_ref[sub * (tm // TQ) + cb] = qa[:, cb * TQ:(cb + 1) * TQ]

    vrow = lax.broadcasted_iota(jnp.int32, (N_HEADS * V_ROWS, tm), 0) % V_ROWS
    ones_rows = (vrow >= HEAD_DIM).astype(F32)
    va = (zt(_T_VA, N_HEADS * V_ROWS) + ones_rows).astype(BF16)
    vd = (zt(_T_VD, N_HEADS * V_ROWS) + ones_rows).astype(BF16)
    for cb in range(tm // TK):
        va_ref[sub * (tm // TK) + cb] = va[:, cb * TK:(cb + 1) * TK]
        vd_ref[sub * (tm // TK) + cb] = vd[:, cb * TK:(cb + 1) * TK]


def _proj_constants(tm):
    def block_diag(n, blk):
        idx = np.arange(n) // blk
        return (idx[:, None] == idx[None, :]).astype(np.float32)

    ltri = np.tril(np.ones((tm, tm), np.float32))
    ek = np.zeros((128, N_HEADS * FOX_PAD), np.float32)
    eq = np.zeros((N_HEADS * FOX_PAD, 128), np.float32)
    onek = np.zeros((1, N_HEADS * FOX_PAD), np.float32)
    oneq = np.zeros((N_HEADS * FOX_PAD, 1), np.float32)
    for hd in range(N_HEADS):
        base = hd * FOX_PAD + HEAD_DIM
        for p in range(3):
            ek[p * N_HEADS + hd, base + p] = -1.0
            eq[base + 3 + p, p * N_HEADS + hd] = 1.0
        onek[0, base + 3:base + 6] = 1.0
        oneq[base:base + 3, 0] = 1.0
    return dict(
        bd32=jnp.asarray(block_diag(GROUP_W, DIFF_QK_DIM), BF16),
        ltri=jnp.asarray(ltri, BF16), ek=jnp.asarray(ek, BF16), eq=jnp.asarray(eq, BF16),
        onek=jnp.asarray(onek), oneq=jnp.asarray(oneq))


def _proj_weights(w_in):
    d = w_in.shape[0]
    c0 = 0
    qa = w_in[:, c0:c0 + GROUP_W]; c0 += GROUP_W
    ka = w_in[:, c0:c0 + GROUP_W]; c0 += GROUP_W
    va = w_in[:, c0:c0 + GROUP_W]; c0 += GROUP_W
    conv = w_in[:, c0:c0 + 2 * GROUP_W]; c0 += 2 * GROUP_W
    gmlp = w_in[:, c0:c0 + 2 * GROUP_W]; c0 += 2 * GROUP_W
    qd = w_in[:, c0:c0 + GROUP_W]; c0 += GROUP_W
    kd = w_in[:, c0:c0 + GROUP_W]; c0 += GROUP_W
    vd = w_in[:, c0:c0 + GROUP_W]; c0 += GROUP_W
    wf = w_in[:, c0:c0 + N_HEADS]

    def pad_heads(w, width):
        w = w.reshape(d, N_HEADS, HEAD_DIM)
        w = jnp.pad(w, ((0, 0), (0, 0), (0, width - HEAD_DIM)))
        return w.reshape(d, N_HEADS * width)

    w_rm = jnp.concatenate(
        [ka, conv, gmlp, pad_heads(kd, FOX_PAD), jnp.pad(wf, ((0, 0), (0, 128 - N_HEADS)))], axis=1)
    w_t = jnp.concatenate(
        [qa, pad_heads(va, V_ROWS), pad_heads(qd, FOX_PAD), pad_heads(vd, V_ROWS)], axis=1).T
    return w_rm.astype(BF16), w_t.astype(BF16)


def _proj_call(h, mod, g, w_rm, w_t, consts, diff_qk_g, fox_qk_g, fox_fb,
               conv_w, conv_b, conv_ng, conv_nb, gmlp_ng, gmlp_nb, gmlp_ws, gmlp_bs, *, layer, seq):
    n, d = h.shape
    sub_tm = TM_PROJ
    tm = PROJ_SUB * sub_tm
    tiles_per_seq = seq // tm
    nqb, nkb = tm // TQ, tm // TK

    diff_scale = DIFF_QK_DIM ** -0.5 * LOG2E
    fox_scale = HEAD_DIM ** -0.5 * LOG2E
    gka = jnp.tile(diff_qk_g[1], 2 * N_HEADS).reshape(1, GROUP_W)
    gqa = (jnp.tile(diff_qk_g[0], 2 * N_HEADS) * diff_scale).reshape(GROUP_W, 1)
    pad = jnp.zeros((FOX_PAD - HEAD_DIM,), F32)
    gkd = jnp.tile(jnp.concatenate([fox_qk_g[1], pad]), N_HEADS).reshape(1, N_HEADS * FOX_PAD)
    gqd = jnp.tile(jnp.concatenate([fox_qk_g[0] * fox_scale, pad]), N_HEADS).reshape(N_HEADS * FOX_PAD, 1)
    fb = jnp.pad(fox_fb, (0, 128 - N_HEADS)).reshape(1, 128)
    cw = jnp.pad(conv_w, ((0, 32 - CONV_KERNEL), (0, 0)))
    gws = gmlp_ws.reshape(N_HEADS * CHUNK, CHUNK)
    gbs = jnp.repeat(gmlp_bs.T, HEAD_DIM, axis=1)

    full = lambda a: _resident(a.shape)
    row = lambda i: (i, 0)
    small = [consts["bd32"], consts["ltri"], consts["ek"], consts["eq"],
             gka, gqa, gkd, gqd, consts["onek"], consts["oneq"], fb,
             cw, conv_b.reshape(1, -1), conv_ng.reshape(1, -1), conv_nb.reshape(1, -1),
             gmlp_ng.reshape(1, -1), gmlp_nb.reshape(1, -1), gws, gbs]
    in_specs = [
        pl.BlockSpec((tm, d), row),
        pl.BlockSpec((1, N_MOD, d), lambda i: (i // tiles_per_seq, 0, 0)),
        full(g.reshape(1, d)),
        _resident_layer(w_rm.shape, layer), _resident_layer(w_t.shape, layer),
    ] + [full(a) for a in small]
    blk3 = lambda i: (i, 0, 0)
    out_specs = [
        pl.BlockSpec((tm, GROUP_W), row),
        pl.BlockSpec((nqb, GROUP_W, TQ), blk3),
        pl.BlockSpec((nkb, N_HEADS * V_ROWS, TK), blk3),
        pl.BlockSpec((tm, N_HEADS * FOX_PAD), row),
        pl.BlockSpec((nqb, N_HEADS * FOX_PAD, TQ), blk3),
        pl.BlockSpec((nkb, N_HEADS * V_ROWS, TK), blk3),
        pl.BlockSpec((tm, GROUP_W), row),
        pl.BlockSpec((tm, GROUP_W), row),
    ]
    out_shape = [
        jax.ShapeDtypeStruct((n, GROUP_W), BF16),
        jax.ShapeDtypeStruct((n // TQ, GROUP_W, TQ), BF16),
        jax.ShapeDtypeStruct((n // TK, N_HEADS * V_ROWS, TK), BF16),
        jax.ShapeDtypeStruct((n, N_HEADS * FOX_PAD), BF16),
        jax.ShapeDtypeStruct((n // TQ, N_HEADS * FOX_PAD, TQ), BF16),
        jax.ShapeDtypeStruct((n // TK, N_HEADS * V_ROWS, TK), BF16),
        jax.ShapeDtypeStruct((n, GROUP_W), BF16),
        jax.ShapeDtypeStruct((n, GROUP_W), BF16),
    ]
    return pl.pallas_call(
        functools.partial(_proj_kernel, steps_per_seq=tiles_per_seq),
        grid=(n // tm,),
        in_specs=in_specs,
        out_specs=out_specs,
        out_shape=out_shape,
        scratch_shapes=[pltpu.VMEM((CONV_HALO + sub_tm + CONV_TAIL, GROUP_W), F32),
                        pltpu.VMEM((8, 128), F32)],
        compiler_params=pltpu.CompilerParams(
            dimension_semantics=("arbitrary",), vmem_limit_bytes=VMEM_LIMIT),
        name="in_proj",
    )(h, mod, g.reshape(1, d), w_rm, w_t, *small)


def _attend(n_chains, q_of, k_of, v_of, s_refs, m_ref, acc_ref, i):
    m_ref[...] = jnp.full(m_ref.shape, NEG_INF, F32)
    acc_ref[...] = jnp.zeros(acc_ref.shape, F32)
    all_q, upper_q = slice(0, TQ), slice(TK, TQ)

    def scores(j, s_ref, ch, qs):
        s_ref[ch, :, qs] = _dot(k_of(ch, j), q_of(ch)[:, qs])

    def update(j, s_ref, ch, masked, qs):
        width = qs.stop - qs.start
        s = s_ref[ch, :, qs]
        if masked:
            kp = lax.broadcasted_iota(jnp.int32, (TK, width), 0) + (j * TK - i * TQ - qs.start)
            s = jnp.where(kp <= lax.broadcasted_iota(jnp.int32, (TK, width), 1), s, NEG_INF)
        m_full = m_ref[ch]
        m_old = m_full[:, qs]
        m_new = jnp.maximum(m_old, jnp.max(s, axis=0, keepdims=True))
        p = jnp.exp2(s - m_new).astype(BF16)
        acc_ref[ch, :, qs] = acc_ref[ch, :, qs] * jnp.exp2(m_old - m_new) + _dot(v_of(ch, j), p)
        m_ref[ch] = m_new if width == TQ else jnp.concatenate([m_full[:, :qs.start], m_new], axis=1)

    def step(j_next, s_next, j_cur, s_cur, masked, qs_next=all_q, qs_cur=all_q):
        for ch in range(n_chains):
            if j_next is not None:
                scores(j_next, s_next, ch, qs_next)
            update(j_cur, s_cur, ch, masked, qs_cur)

    n_pairs = i
    for ch in range(n_chains):
        scores(0, s_refs[0], ch, all_q)

    def body(jj, carry):
        step(2 * jj + 1, s_refs[1], 2 * jj, s_refs[0], False)
        step(2 * jj + 2, s_refs[0], 2 * jj + 1, s_refs[1], False)
        return carry

    lax.fori_loop(0, n_pairs, body, 0)
    step(2 * n_pairs + 1, s_refs[1], 2 * n_pairs, s_refs[0], True, qs_next=upper_q)
    step(None, None, 2 * n_pairs + 1, s_refs[1], True, qs_cur=upper_q)


def _normalised(acc_ref, ch):
    return acc_ref[ch, 0:HEAD_DIM, :] * (1.0 / acc_ref[ch, HEAD_DIM:HEAD_DIM + 1, :])


def _diff_attn_kernel(q_ref, k_ref, v_ref, lam_ref, og_ref, o_ref,
                      qm_ref, s0_ref, s1_ref, m_ref, acc_ref, ot_ref, *, lam_init):
    i = pl.program_id(1)
    q_t = q_ref[0]
    rows = lax.broadcasted_iota(jnp.int32, (GROUP_W, TQ), 0)
    for ch in range(2 * N_HEADS):
        lo = ch * DIFF_QK_DIM
        qm_ref[ch] = jnp.where((rows >= lo) & (rows < lo + DIFF_QK_DIM), q_t, jnp.zeros_like(q_t))
    _attend(2 * N_HEADS, lambda ch: qm_ref[ch], lambda ch, j: k_ref[j],
            lambda ch, j: v_ref[j, (ch // 2) * V_ROWS:(ch // 2 + 1) * V_ROWS, :],
            (s0_ref, s1_ref), m_ref, acc_ref, i)
    lv = lam_ref[...]
    lam = (jnp.exp(jnp.sum(lv[0:1, :] * lv[1:2, :], axis=-1, keepdims=True))
           - jnp.exp(jnp.sum(lv[2:3, :] * lv[3:4, :], axis=-1, keepdims=True)) + lam_init)
    for hd in range(N_HEADS):
        o = _normalised(acc_ref, 2 * hd) - lam * _normalised(acc_ref, 2 * hd + 1)
        ms = jnp.mean(o * o, axis=0, keepdims=True)
        ot_ref[hd * HEAD_DIM:(hd + 1) * HEAD_DIM, :] = o * lax.rsqrt(ms + EPS) * og_ref[...]
    o_ref[...] = ot_ref[...].T.astype(BF16)


def _fox_attn_kernel(q_ref, k_ref, v_ref, o_ref, s0_ref, s1_ref, m_ref, acc_ref, ot_ref):
    i = pl.program_id(1)
    _attend(N_HEADS, lambda ch: q_ref[0, ch * FOX_PAD:(ch + 1) * FOX_PAD, :],
            lambda ch, j: k_ref[j, :, ch * FOX_PAD:(ch + 1) * FOX_PAD],
            lambda ch, j: v_ref[j, ch * V_ROWS:(ch + 1) * V_ROWS, :],
            (s0_ref, s1_ref), m_ref, acc_ref, i)
    for hd in range(N_HEADS):
        ot_ref[hd * HEAD_DIM:(hd + 1) * HEAD_DIM, :] = _normalised(acc_ref, hd)
    o_ref[...] = ot_ref[...].T.astype(BF16)


def _attn_call(kernel, q_t, k, v_t, extra, *, n_chains, masked_q, batch, seq, name):
    nq, nk = seq // TQ, seq // TK
    feat = q_t.shape[1]
    k = k.reshape(batch * nk, TK, feat)
    scratch = [pltpu.VMEM((n_chains, feat, TQ), BF16)] if masked_q else []
    scratch += [pltpu.VMEM((n_chains, TK, TQ), F32), pltpu.VMEM((n_chains, TK, TQ), F32),
                pltpu.VMEM((n_chains, 1, TQ), F32), pltpu.VMEM((n_chains, V_ROWS, TQ), F32),
                pltpu.VMEM((GROUP_W, TQ), F32)]
    in_specs = [
        pl.BlockSpec((1, feat, TQ), lambda b, i: (b * nq + i, 0, 0)),
        pl.BlockSpec((nk, TK, feat), lambda b, i: (b, 0, 0)),
        pl.BlockSpec((nk, N_HEADS * V_ROWS, TK), lambda b, i: (b, 0, 0)),
    ] + [pl.BlockSpec(a.shape, lambda b, i, _nd=a.ndim: (0,) * _nd) for a in extra]
    return pl.pallas_call(
        kernel,
        grid=(batch, nq),
        in_specs=in_specs,
        out_specs=pl.BlockSpec((TQ, GROUP_W), lambda b, i: (b * nq + i, 0)),
        out_shape=jax.ShapeDtypeStruct((batch * seq, GROUP_W), BF16),
        scratch_shapes=scratch,
        compiler_params=pltpu.CompilerParams(
            dimension_semantics=("arbitrary", "arbitrary"), vmem_limit_bytes=VMEM_LIMIT),
        name=name,
    )(q_t, k, v_t, *extra)


def kernel(x, c, ada_w, ada_b, norm_g, ffn1_w_in, ffn1_w_out, ffn2_w_in, ffn2_w_out, w_in, w_out, diff_qk_g, diff_lambda, diff_out_g, conv_w, conv_b, conv_norm_g, conv_norm_b, gmlp_norm_g, gmlp_norm_b, gmlp_ws, gmlp_bs, fox_qk_g, fox_fb):
    batch, seq, d = x.shape
    depth = ada_w.shape[0]
    assert seq % (PROJ_SUB * TM_PROJ) == 0 and seq % TM_FFN == 0 and TM_PROJ % TQ == 0
    assert w_in.shape[-1] == 10 * GROUP_W + N_HEADS and w_out.shape[1] == 4 * GROUP_W
    assert ffn1_w_out.shape[1] % TF_FFN == 0

    mods = _ada_call(c, ada_w, ada_b)
    w1_in, w1_out, w2_in, w2_out, w_mix = (
        a.astype(BF16) for a in (ffn1_w_in, ffn1_w_out, ffn2_w_in, ffn2_w_out, w_out))
    w_rm, w_t = jax.vmap(_proj_weights)(w_in)
    consts = _proj_constants(TM_PROJ)
    h = x.reshape(batch * seq, d)
    for l in range(depth):
        mod = mods[l]
        h = _ffn_call(h, mod, norm_g[l, 0], w1_in, w1_out, layer=l, seq=seq, mod_row=0)
        ka, qa_t, va_t, kd, qd_t, vd_t, o_b, o_c = _proj_call(
            h, mod, norm_g[l, 1], w_rm, w_t, consts, diff_qk_g[l], fox_qk_g[l], fox_fb[l],
            conv_w[l], conv_b[l], conv_norm_g[l], conv_norm_b[l],
            gmlp_norm_g[l], gmlp_norm_b[l], gmlp_ws[l], gmlp_bs[l], layer=l, seq=seq)
        lam_init = 0.8 - 0.6 * math.exp(-0.3 * l)
        og = (diff_out_g[l] * (1.0 - lam_init)).reshape(HEAD_DIM, 1)
        o_a = _attn_call(functools.partial(_diff_attn_kernel, lam_init=lam_init),
                         qa_t, ka, va_t, [diff_lambda[l], og], n_chains=2 * N_HEADS, masked_q=True,
                         batch=batch, seq=seq, name="diff_attn")
        o_d = _attn_call(_fox_attn_kernel, qd_t, kd, vd_t, [], n_chains=N_HEADS, masked_q=False,
                         batch=batch, seq=seq, name="fox_attn")
        h = _ffn_call(h, mod, norm_g[l, 2], w2_in, w2_out, layer=l, seq=seq, mod_row=6,
                      mix=(o_a, o_b, o_c, o_d, w_mix))
    return h.reshape(batch, seq, d)
```
